```python
import functools
import jax, jax.numpy as jnp
from jax import lax
import numpy as np

D_MODEL = 1024
BATCH = 8
SEQ = 8192
DEPTH = 2
DEC_BATCH = 8
DEC_SEQ = 16
PAST_LEN = 2048

CHUNK = 64
N_META = 16
SB_HEADS = 8
SB_DIM = 64
SB_QBLOCK = 128
SB_W = SB_HEADS * SB_DIM
GLA_HEADS = 4
GLA_DK = D_MODEL // 2 // GLA_HEADS
GLA_DV = D_MODEL // GLA_HEADS
GLA_KW = GLA_HEADS * GLA_DK
GLA_VW = GLA_HEADS * GLA_DV
GLA_RANK = 16
GLA_TAU = 16.0
PEER_HEADS = 8
PEER_NKEYS = 128
PEER_NEXPERTS = PEER_NKEYS * PEER_NKEYS
PEER_DKEY = 256
PEER_HALF = PEER_DKEY // 2
PEER_TOPK = 16
PEER_BLOCK = 256
DN_ALPHA = float((2 * DEPTH) ** 0.25)
DN_BETA = float((8 * DEPTH) ** -0.25)
LN_EPS = 1e-5
RMS_EPS = 1e-6

IN_SPLITS = (SB_W, SB_W, SB_W, GLA_KW, GLA_KW, GLA_VW, GLA_VW, GLA_RANK, D_MODEL, D_MODEL)
IN_WIDTH = int(sum(IN_SPLITS))
SPLIT_POINTS = tuple(int(s) for s in np.cumsum(IN_SPLITS)[:-1])

kernel_name = 'hybrid_stickbreak_gla_peer_stream_step'

F32 = jnp.float32


def layer_norm(x, g, b):
    xf = x.astype(F32)
    mu = jnp.mean(xf, axis=-1, keepdims=True)
    var = jnp.mean(jnp.square(xf - mu), axis=-1, keepdims=True)
    return ((xf - mu) * lax.rsqrt(var + LN_EPS) * g + b).astype(x.dtype)


def post_norm(x, y, g, b):
    return layer_norm(DN_ALPHA * x + y, g, b)


def sb_block(q, k, v, q_pos, k_pos):
    z = jnp.einsum('bqhd,bkhd->bhqk', q.astype(F32), k.astype(F32)) * (SB_DIM ** -0.5)
    mask = k_pos[None, :] < q_pos[:, None]
    log_keep = jnp.where(mask, jax.nn.log_sigmoid(-z), 0.0)
    after = lax.cumsum(log_keep, axis=3, reverse=True) - log_keep
    w = jnp.where(mask, jnp.exp(jax.nn.log_sigmoid(z) + after), 0.0)
    return jnp.einsum('bhqk,bkhd->bqhd', w, v.astype(F32))


def stick_breaking(q, k, v, q_pos, k_pos):
    B, Tq = q.shape[0], q.shape[1]
    blk = min(SB_QBLOCK, Tq)
    nb = -(-Tq // blk)
    pad = nb * blk - Tq
    qp = jnp.pad(q, ((0, 0), (0, pad), (0, 0), (0, 0)))
    pp = jnp.concatenate([q_pos, q_pos[-1] + 1 + jnp.arange(pad, dtype=q_pos.dtype)])
    qb = qp.reshape(B, nb, blk, SB_HEADS, SB_DIM).transpose(1, 0, 2, 3, 4)
    pb = pp.reshape(nb, blk)
    out = lax.map(lambda a: sb_block(a[0], k, v, a[1], k_pos), (qb, pb))
    return out.transpose(1, 0, 2, 3, 4).reshape(B, nb * blk, SB_HEADS, SB_DIM)[:, :Tq]


def gla_scan(q, k, v, log_a, s0):
    B, L = q.shape[0], q.shape[1]
    nc = L // CHUNK

    def to_chunks(t):
        return t.astype(F32).reshape(B, nc, CHUNK, *t.shape[2:]).swapaxes(0, 1)

    causal = jnp.tril(jnp.ones((CHUNK, CHUNK), dtype=bool))[None, :, :, None, None]

    def step(S, inp):
        qc, kc, vc, gc = inp
        b = jnp.cumsum(gc, axis=1)
        inter = jnp.einsum('bthk,bhkv->bthv', qc * jnp.exp(b), S)
        diff = b[:, :, None] - b[:, None, :]
        decay = jnp.where(causal, jnp.exp(jnp.where(causal, diff, 0.0)), 0.0)
        scores = jnp.einsum('bthk,bshk,btshk->bhts', qc, kc, decay)
        intra = jnp.einsum('bhts,bshv->bthv', scores, vc)
        last = b[:, -1]
        S = jnp.exp(last)[..., None] * S + jnp.einsum(
            'bshk,bshv->bhkv', kc * jnp.exp(last[:, None] - b), vc)
        return S, inter + intra

    S, o = lax.scan(step, s0.astype(F32),
                    (to_chunks(q), to_chunks(k), to_chunks(v), to_chunks(log_a)))
    return o.swapaxes(0, 1).reshape(B, L, GLA_HEADS, GLA_DV), S


def mixer_project(h, w_in, w_gk2, b_gk):
    B, T = h.shape[0], h.shape[1]
    p = h @ w_in
    qa, ka, va, qb, kb, vb, rb, glr, ga, gb = jnp.split(p, SPLIT_POINTS, axis=-1)
    log_a = jax.nn.log_sigmoid((glr @ w_gk2 + b_gk).astype(F32)) / GLA_TAU
    sbs = (B, T, SB_HEADS, SB_DIM)
    return (qa.reshape(sbs), ka.reshape(sbs), va.reshape(sbs),
            qb.reshape(B, T, GLA_HEADS, GLA_DK) * (GLA_DK ** -0.5),
            kb.reshape(B, T, GLA_HEADS, GLA_DK),
            vb.reshape(B, T, GLA_HEADS, GLA_DV),
            log_a.reshape(B, T, GLA_HEADS, GLA_DK), rb, ga, gb)


def mixer_merge(h, sb_out, gla_out, rb, ga, gb, gla_norm_g, w_pa, w_pb, w_o):
    B, T = h.shape[0], h.shape[1]
    o = gla_out * lax.rsqrt(jnp.mean(jnp.square(gla_out), axis=-1, keepdims=True) + RMS_EPS) * gla_norm_g
    gla_y = (o.reshape(B, T, GLA_VW) * jax.nn.silu(rb.astype(F32))).astype(h.dtype) @ w_pb
    sb_y = sb_out.reshape(B, T, SB_W).astype(h.dtype) @ w_pa
    merged = jax.nn.sigmoid(ga) * sb_y + jax.nn.sigmoid(gb) * gla_y
    return merged @ w_o


def peer_block(xb, w_q, sub_keys, u, v):
    n = xb.shape[0]
    q = (xb @ w_q).astype(F32).reshape(n, PEER_HEADS, 2, PEER_HALF)
    s = jnp.einsum('nhpd,pkd->nhpk', q, sub_keys.astype(F32))
    sv, si = lax.top_k(s, PEER_TOPK)
    cand = (sv[:, :, 0, :, None] + sv[:, :, 1, None, :]).reshape(n, PEER_HEADS, PEER_TOPK * PEER_TOPK)
    cv, ci = lax.top_k(cand, PEER_TOPK)
    i1 = jnp.take_along_axis(si[:, :, 0], ci // PEER_TOPK, axis=-1)
    i2 = jnp.take_along_axis(si[:, :, 1], ci % PEER_TOPK, axis=-1)
    e = i1 * PEER_NKEYS + i2
    g = jax.nn.softmax(cv, axis=-1)
    ue = jnp.take(u, e, axis=0).astype(F32)
    a = jax.nn.gelu(jnp.einsum('nd,nhkd->nhk', xb.astype(F32), ue), approximate=False) * g
    ve = jnp.take(v, e, axis=0).astype(F32)
    return jnp.einsum('nhk,nhkd->nd', a, ve).astype(xb.dtype)


def peer(h, w_q, sub_keys, u, v):
    shp = h.shape
    xf = h.reshape(-1, D_MODEL)
    n = xf.shape[0]
    blk = min(PEER_BLOCK, n)
    nb = -(-n // blk)
    xp = jnp.pad(xf, ((0, nb * blk - n), (0, 0))).reshape(nb, blk, D_MODEL)
    out = lax.map(lambda xb: peer_block(xb, w_q, sub_keys, u, v), xp)
    return out.reshape(nb * blk, D_MODEL)[:n].reshape(shp)


def setup_inputs(seed: int = 0) -> dict:
    key = jax.random.key(seed)
    ks = jax.random.split(key, 26)

    def nrm(k, shape, scale):
        return jax.random.normal(k, shape, F32) * scale

    D = D_MODEL
    return {
        'x_prompt': nrm(ks[0], (BATCH, SEQ, D), 1.0),
        'x_sample': nrm(ks[1], (DEC_BATCH, DEC_SEQ, D), 1.0),
        'cache_k': nrm(ks[2], (DEPTH, DEC_BATCH, PAST_LEN, SB_HEADS, SB_DIM), 1.0),
        'cache_v': nrm(ks[3], (DEPTH, DEC_BATCH, PAST_LEN, SB_HEADS, SB_DIM), 1.0),
        'state_gla': nrm(ks[4], (DEPTH, DEC_BATCH, GLA_HEADS, GLA_DK, GLA_DV), 1.0),
        'meta': nrm(ks[5], (N_META, D), 1.0),
        'ln_in_g': 1.0 + nrm(ks[6], (D,), 0.02),
        'ln_in_b': nrm(ks[7], (D,), 0.02),
        'w_in': nrm(ks[8], (DEPTH, D, IN_WIDTH), D ** -0.5),
        'w_gk2': nrm(ks[9], (DEPTH, GLA_RANK, GLA_KW), GLA_RANK ** -0.5),
        'b_gk': nrm(ks[10], (DEPTH, GLA_KW), 0.1),
        'gla_norm_g': 1.0 + nrm(ks[11], (DEPTH, GLA_DV), 0.02),
        'w_pa': nrm(ks[12], (DEPTH, SB_W, D), SB_W ** -0.5),
        'w_pb': nrm(ks[13], (DEPTH, GLA_VW, D), GLA_VW ** -0.5),
        'w_o': nrm(ks[14], (DEPTH, D, D), DN_BETA * D ** -0.5),
        'ln1_g': 1.0 + nrm(ks[15], (DEPTH, D), 0.02),
        'ln1_b': nrm(ks[16], (DEPTH, D), 0.02),
        'peer_wq': nrm(ks[17], (DEPTH, D, PEER_HEADS * PEER_DKEY), D ** -0.5),
        'peer_subkeys': nrm(ks[18], (DEPTH, 2, PEER_NKEYS, PEER_HALF), PEER_HALF ** -0.5),
        'peer_u': nrm(ks[19], (DEPTH, PEER_NEXPERTS, D), D ** -0.5),
        'peer_v': nrm(ks[20], (DEPTH, PEER_NEXPERTS, D), DN_BETA * PEER_HEADS ** -0.5),
        'ln2_g': 1.0 + nrm(ks[21], (DEPTH, D), 0.02),
        'ln2_b': nrm(ks[22], (DEPTH, D), 0.02),
    }


def reference(x_prompt, x_sample, cache_k, cache_v, state_gla, meta, ln_in_g, ln_in_b,
              w_in, w_gk2, b_gk, gla_norm_g, w_pa, w_pb, w_o, ln1_g, ln1_b,
              peer_wq, peer_subkeys, peer_u, peer_v, ln2_g, ln2_b):
    B, S_len = x_prompt.shape[0], x_prompt.shape[1]
    Bd, T = x_sample.shape[0], x_sample.shape[1]
    L = N_META + S_len
    keep = min(S_len, PAST_LEN)

    meta_rows = jnp.broadcast_to(meta[None].astype(x_prompt.dtype), (B, N_META, D_MODEL))
    hp = layer_norm(jnp.concatenate([meta_rows, x_prompt], axis=1), ln_in_g, ln_in_b)
    hs = layer_norm(x_sample, ln_in_g, ln_in_b)

    p_pos = jnp.arange(L)
    s_kpos = jnp.arange(N_META + PAST_LEN + T)
    s_qpos = s_kpos[N_META + PAST_LEN:]
    front = (-N_META) % CHUNK
    back = (-(front + L)) % CHUNK
    s_back = (-T) % CHUNK

    def pad_p(t):
        return jnp.pad(t, ((0, 0), (front, back), (0, 0), (0, 0)))

    def pad_s(t):
        return jnp.pad(t, ((0, 0), (0, s_back), (0, 0), (0, 0)))

    nk_p, nv_p, st_p, nk_s, nv_s, st_s = [], [], [], [], [], []
    for l in range(DEPTH):
        qa, ka, va, qb, kb, vb, la, rb, ga, gb = mixer_project(hp, w_in[l], w_gk2[l], b_gk[l])
        sb = stick_breaking(qa, ka, va, p_pos, p_pos)
        s0 = jnp.zeros((B, GLA_HEADS, GLA_DK, GLA_DV), F32)
        go, sp = gla_scan(pad_p(qb), pad_p(kb), pad_p(vb), pad_p(la), s0)
        go = go[:, front:front + L]
        y = mixer_merge(hp, sb, go, rb, ga, gb, gla_norm_g[l], w_pa[l], w_pb[l], w_o[l])
        hp = post_norm(hp, y, ln1_g[l], ln1_b[l])
        hp = post_norm(hp, peer(hp, peer_wq[l], peer_subkeys[l], peer_u[l], peer_v[l]), ln2_g[l], ln2_b[l])
        nk_p.append(ka[:, L - keep:])
        nv_p.append(va[:, L - keep:])
        st_p.append(sp.astype(state_gla.dtype))
        meta_k = jnp.broadcast_to(ka[:1, :N_META], (Bd, N_META, SB_HEADS, SB_DIM))
        meta_v = jnp.broadcast_to(va[:1, :N_META], (Bd, N_META, SB_HEADS, SB_DIM))

        qa2, ka2, va2, qb2, kb2, vb2, la2, rb2, ga2, gb2 = mixer_project(hs, w_in[l], w_gk2[l], b_gk[l])
        k_all = jnp.concatenate([meta_k, cache_k[l].astype(ka2.dtype), ka2], axis=1)
        v_all = jnp.concatenate([meta_v, cache_v[l].astype(va2.dtype), va2], axis=1)
        sb2 = stick_breaking(qa2, k_all, v_all, s_qpos, s_kpos)
        go2, ss = gla_scan(pad_s(qb2), pad_s(kb2), pad_s(vb2), pad_s(la2), state_gla[l])
        go2 = go2[:, :T]
        y2 = mixer_merge(hs, sb2, go2, rb2, ga2, gb2, gla_norm_g[l], w_pa[l], w_pb[l], w_o[l])
        hs = post_norm(hs, y2, ln1_g[l], ln1_b[l])
        hs = post_norm(hs, peer(hs, peer_wq[l], peer_subkeys[l], peer_u[l], peer_v[l]), ln2_g[l], ln2_b[l])
        nk_s.append(ka2)
        nv_s.append(va2)
        st_s.append(ss.astype(state_gla.dtype))

    y_prompt = hp[:, N_META:]
    y_sample = hs
    new_k_prompt = jnp.stack(nk_p)
    new_v_prompt = jnp.stack(nv_p)
    state_prompt = jnp.stack(st_p)
    new_k_sample = jnp.stack(nk_s)
    new_v_sample = jnp.stack(nv_s)
    state_sample = jnp.stack(st_s)
    return (y_prompt, y_sample, new_k_prompt, new_v_prompt, state_prompt, new_k_sample, new_v_sample, state_sample)
```

```python
import functools

import jax
import jax.numpy as jnp
from jax import lax
from jax.experimental import pallas as pl
from jax.experimental.pallas import tpu as pltpu

F32 = jnp.float32
BF16 = jnp.bfloat16

D_MODEL = 1024
DEPTH = 2
N_META = 16
SB_HEADS = 8
SB_DIM = 64
SB_W = SB_HEADS * SB_DIM
GLA_HEADS = 4
GLA_DK = 128
GLA_DV = 256
GLA_KW = GLA_HEADS * GLA_DK
GLA_VW = GLA_HEADS * GLA_DV
GLA_RANK = 16
GLA_TAU = 16.0
PEER_HEADS = 8
PEER_HALF = 128
PEER_TOPK = 16
DN_ALPHA = float((2 * DEPTH) ** 0.25)
LN_EPS = 1e-5
RMS_EPS = 1e-6

LANES = 128
SUBLANES = 8
VMEM_LIMIT_BYTES = 56 * 1024 * 1024

ATTN_BLOCK = 128
GLA_CHUNK = 64
PEER_GROUP = 16
EXP_ZERO_BELOW = -104.0

_NT = (((1,), (1,)), ((), ()))
_TN = (((0,), (0,)), ((), ()))


def _cparams(sem):
    return pltpu.CompilerParams(dimension_semantics=sem, vmem_limit_bytes=VMEM_LIMIT_BYTES)


def _row_tile(n):
    for t in (512, 256, 128):
        if n % t == 0:
            return t
    raise ValueError(f"token count {n} is not a multiple of 128")


def _layer_norm(x, g, b):
    mu = jnp.mean(x, axis=-1, keepdims=True)
    xc = x - mu
    var = jnp.mean(xc * xc, axis=-1, keepdims=True)
    return xc * lax.rsqrt(var + LN_EPS) * g + b


def _softplus(x):
    return jnp.maximum(x, 0.0) + jnp.log(1.0 + jnp.exp(-jnp.abs(x)))


def _sigmoid(x):
    return 1.0 / (1.0 + jnp.exp(-x))


def _ln_kernel(x_ref, g_ref, b_ref, o_ref):
    o_ref[...] = _layer_norm(x_ref[...], g_ref[...], b_ref[...])


def _ln_call(x, g, b):
    n, d = x.shape
    tm = _row_tile(n)
    row = pl.BlockSpec((tm, d), lambda i: (i, 0))
    vec = pl.BlockSpec((1, d), lambda i: (0, 0))
    return pl.pallas_call(
        _ln_kernel, grid=(n // tm,), in_specs=[row, vec, vec], out_specs=row,
        out_shape=jax.ShapeDtypeStruct((n, d), F32), compiler_params=_cparams(("parallel",)),
        name="ln_in")(x, g.reshape(1, d), b.reshape(1, d))


_SEC = (0, 512, 1024, 1536, 2048, 2560, 3584, 4608, 5632, 6656)


def _inproj_kernel(h_ref, w_ref, wg_ref, wgk2_ref, bgk_ref,
                   qa_ref, kab_ref, vab_ref, ka_ref, va_ref, qb_ref, kb_ref, vb_ref, la_ref,
                   rb_ref, ga_ref, gb_ref):
    hb = h_ref[...].astype(BF16)

    def sec(i):
        return jnp.dot(hb, w_ref[:, _SEC[i]:_SEC[i + 1]], preferred_element_type=F32)

    qa_ref[...] = (sec(0) * (SB_DIM ** -0.5)).astype(BF16)
    ka = sec(1)
    ka_ref[...] = ka
    kab_ref[...] = ka.astype(BF16)
    va = sec(2)
    va_ref[...] = va
    vab_ref[...] = va.astype(BF16)
    qb_ref[...] = sec(3) * (GLA_DK ** -0.5)
    kb_ref[...] = sec(4)
    vb_ref[...] = sec(5).astype(BF16)
    rb_ref[...] = sec(6)
    ga_ref[...] = sec(7)
    gb_ref[...] = sec(8)
    glr = jnp.dot(hb, wg_ref[...], preferred_element_type=F32)
    x = jnp.dot(glr, wgk2_ref[...], preferred_element_type=F32) + bgk_ref[...]
    la_ref[...] = -_softplus(-x) * (1.0 / GLA_TAU)


def _inproj_call(h, w_main, w_glr, w_gk2p, b_gk):
    n, d = h.shape
    tm = min(_row_tile(n), 256)
    row = lambda w: pl.BlockSpec((tm, w), lambda i: (i, 0))
    full = lambda a: pl.BlockSpec(a.shape, lambda i: (0,) * a.ndim)
    outs = [(SB_W, BF16), (SB_W, BF16), (SB_W, BF16), (SB_W, F32), (SB_W, F32),
            (GLA_KW, F32), (GLA_KW, F32), (GLA_VW, BF16), (GLA_KW, F32),
            (GLA_VW, F32), (D_MODEL, F32), (D_MODEL, F32)]
    return pl.pallas_call(
        _inproj_kernel, grid=(n // tm,),
        in_specs=[row(d), full(w_main), full(w_glr), full(w_gk2p), full(b_gk)],
        out_specs=[row(w) for w, _ in outs],
        out_shape=[jax.ShapeDtypeStruct((n, w), t) for w, t in outs],
        compiler_params=_cparams(("parallel",)), name="in_proj")(h, w_main, w_glr, w_gk2p, b_gk)


def _sb_attn_kernel(q_ref, k_ref, v_ref, o_ref, *, q_off, nk):
    tq = tk = ATTN_BLOCK
    i = pl.program_id(2)
    q = q_ref[0]
    lane = lax.broadcasted_iota(jnp.int32, (tq, LANES), 1)
    r_iota = lax.broadcasted_iota(jnp.int32, (tq, tk), 0)
    c_iota = lax.broadcasted_iota(jnp.int32, (tq, tk), 1)
    later = (r_iota > c_iota).astype(BF16)
    qpos = q_off + i * tq + r_iota
    j_start = jnp.minimum((q_off + i * tq + tq - 2) // tk, nk - 1)

    def one_head(first):
        qm = jnp.where(lane < SB_DIM if first else lane >= SB_DIM, q, jnp.zeros_like(q))

        def cond(c):
            j, _, _, cmax = c
            return jnp.logical_and(j >= 0, cmax > EXP_ZERO_BELOW)

        def body(c):
            j, carry, acc, _ = c
            off = pl.multiple_of(j * tk, tk)
            kb = k_ref[0, pl.ds(off, tk), :]
            vb = v_ref[0, pl.ds(off, tk), :]
            z = lax.dot_general(qm, kb, _NT, preferred_element_type=F32)
            sp = _softplus(z)
            mask = (off + c_iota) < qpos
            lk = jnp.where(mask, -sp, 0.0)
            hi = lk.astype(BF16)
            lo = (lk - hi.astype(F32)).astype(BF16)
            after = (jnp.dot(hi, later, preferred_element_type=F32)
                     + jnp.dot(lo, later, preferred_element_type=F32) + carry)
            w = jnp.where(mask, jnp.exp(z - sp + after), 0.0).astype(BF16)
            acc = acc + jnp.dot(w, vb, preferred_element_type=F32)
            carry = carry + jnp.sum(lk, axis=1, keepdims=True)
            return j - 1, carry, acc, jnp.max(carry)

        init = (j_start, jnp.zeros((tq, 1), F32), jnp.zeros((tq, LANES), F32), jnp.float32(0.0))
        return lax.while_loop(cond, body, init)[2]

    acc0 = one_head(True)
    acc1 = one_head(False)
    o_ref[0] = jnp.where(lane < SB_DIM, acc0, acc1)


def _sb_attn_call(q, k, v, q_off):
    b, tq_all, _ = q.shape
    tk_all = k.shape[1]
    nq, nk = tq_all // ATTN_BLOCK, tk_all // ATTN_BLOCK
    qspec = pl.BlockSpec((1, ATTN_BLOCK, LANES), lambda bi, p, i: (bi, i, p))
    kspec = pl.BlockSpec((1, tk_all, LANES), lambda bi, p, i: (bi, 0, p))
    return pl.pallas_call(
        functools.partial(_sb_attn_kernel, q_off=q_off, nk=nk),
        grid=(b, SB_W // LANES, nq), in_specs=[qspec, kspec, kspec], out_specs=qspec,
        out_shape=jax.ShapeDtypeStruct((b, tq_all, SB_W), F32),
        compiler_params=_cparams(("parallel", "parallel", "arbitrary")), name="sb_attn")(q, k, v)


def _gla_kernel(q_ref, k_ref, la_ref, v_ref, s0_ref, e_ref, o_ref, st_ref, m_scr, b_scr, k_scr, *, n_valid):
    c = GLA_CHUNK
    ci = pl.program_id(1)

    @pl.when(ci == 0)
    def _():
        st_ref[...] = s0_ref[...]

    r_iota = lax.broadcasted_iota(jnp.int32, (c, c), 0)
    c_iota = lax.broadcasted_iota(jnp.int32, (c, c), 1)
    causal = r_iota >= c_iota
    tril = causal.astype(BF16)
    valid = (ci * c + lax.broadcasted_iota(jnp.int32, (c, GLA_DK), 0)) < n_valid

    for h in range(GLA_HEADS):
        ks = slice(h * GLA_DK, (h + 1) * GLA_DK)
        vs = slice(h * GLA_DV, (h + 1) * GLA_DV)
        la = jnp.where(valid, la_ref[0, :, ks], 0.0)
        k = jnp.where(valid, k_ref[0, :, ks], 0.0)
        q = q_ref[0, :, ks]
        v = v_ref[0, :, vs]
        l1 = la.astype(BF16)
        r1 = la - l1.astype(F32)
        l2 = r1.astype(BF16)
        l3 = (r1 - l2.astype(F32)).astype(BF16)
        b = (jnp.dot(tril, l1, preferred_element_type=F32) + jnp.dot(tril, l2, preferred_element_type=F32)
             + jnp.dot(tril, l3, preferred_element_type=F32))
        b_scr[...] = b
        k_scr[...] = k
        b_last = b[c - 1:c, :]
        st = st_ref[0, h]
        inter = lax.dot_general((q * jnp.exp(b)).astype(BF16), st.astype(BF16), _NT,
                                preferred_element_type=F32)

        def fill(s, carry):
            bs = b_scr[pl.ds(s, 1), :]
            ksr = k_scr[pl.ds(s, 1), :]
            prod = q * jnp.exp(jnp.minimum(b - bs, 0.0)) * ksr
            m_scr[:, pl.ds(pl.multiple_of(s * GLA_DK, GLA_DK), GLA_DK)] = prod.astype(BF16)
            return carry

        lax.fori_loop(0, c, fill, 0)
        scores = jnp.dot(m_scr[...], e_ref[...], preferred_element_type=F32)[:, :c]
        scores = jnp.where(causal, scores, 0.0)
        intra = jnp.dot(scores.astype(BF16), v, preferred_element_type=F32)
        o_ref[0, :, vs] = inter + intra
        kd = (k * jnp.exp(b_last - b)).astype(BF16)
        st_ref[0, h] = st * jnp.exp(b_last) + lax.dot_general(v, kd, _TN, preferred_element_type=F32)


def _gla_call(q, k, la, v, s0t, n_valid):
    b, t, _ = q.shape
    c = GLA_CHUNK
    e = (jnp.arange(c * GLA_DK, dtype=jnp.int32)[:, None] // GLA_DK
         == jnp.arange(LANES, dtype=jnp.int32)[None, :]).astype(BF16)
    kspec = pl.BlockSpec((1, c, GLA_KW), lambda bi, ci: (bi, ci, 0))
    vspec = pl.BlockSpec((1, c, GLA_VW), lambda bi, ci: (bi, ci, 0))
    sspec = pl.BlockSpec((1, GLA_HEADS, GLA_DV, GLA_DK), lambda bi, ci: (bi, 0, 0, 0))
    espec = pl.BlockSpec(e.shape, lambda bi, ci: (0, 0))
    return pl.pallas_call(
        functools.partial(_gla_kernel, n_valid=n_valid),
        grid=(b, t // c), in_specs=[kspec, kspec, kspec, vspec, sspec, espec],
        out_specs=[vspec, sspec],
        out_shape=[jax.ShapeDtypeStruct((b, t, GLA_VW), F32),
                   jax.ShapeDtypeStruct((b, GLA_HEADS, GLA_DV, GLA_DK), F32)],
        scratch_shapes=[pltpu.VMEM((c, c * GLA_DK), BF16), pltpu.VMEM((c, GLA_DK), F32),
                        pltpu.VMEM((c, GLA_DK), F32)],
        compiler_params=_cparams(("parallel", "arbitrary")), name="gla")(q, k, la, v, s0t, e)


def _merge_kernel(h_ref, sb_ref, go_ref, rb_ref, ga_ref, gb_ref, gn_ref, wpa_ref, wpb_ref, wo_ref,
                  g1_ref, b1_ref, out_ref):
    go = go_ref[...]
    parts = []
    for h in range(GLA_HEADS):
        x = go[:, h * GLA_DV:(h + 1) * GLA_DV]
        ms = jnp.mean(x * x, axis=-1, keepdims=True)
        parts.append(x * lax.rsqrt(ms + RMS_EPS) * gn_ref[...])
    o = jnp.concatenate(parts, axis=-1)
    rb = rb_ref[...]
    gated = (o * (rb * _sigmoid(rb))).astype(BF16)
    gla_y = jnp.dot(gated, wpb_ref[...], preferred_element_type=F32)
    sb_y = jnp.dot(sb_ref[...].astype(BF16), wpa_ref[...], preferred_element_type=F32)
    merged = _sigmoid(ga_ref[...]) * sb_y + _sigmoid(gb_ref[...]) * gla_y
    y = jnp.dot(merged.astype(BF16), wo_ref[...], preferred_element_type=F32)
    out_ref[...] = _layer_norm(DN_ALPHA * h_ref[...] + y, g1_ref[...], b1_ref[...])


def _merge_call(h, sb, go, rb, ga, gb, gn, wpa, wpb, wo, g1, b1):
    n, d = h.shape
    tm = min(_row_tile(n), 256)
    row = lambda w: pl.BlockSpec((tm, w), lambda i: (i, 0))
    full = lambda a: pl.BlockSpec(a.shape, lambda i: (0,) * a.ndim)
    args = (h, sb, go, rb, ga, gb, gn, wpa, wpb, wo, g1, b1)
    return pl.pallas_call(
        _merge_kernel, grid=(n // tm,),
        in_specs=[row(d), row(SB_W), row(GLA_VW), row(GLA_VW), row(d), row(d)] + [full(a) for a in args[6:]],
        out_specs=row(d), out_shape=jax.ShapeDtypeStruct((n, d), F32),
        compiler_params=_cparams(("parallel",)), name="merge")(*args)


def _top_values(s, count):
    rows = s.shape[0]
    iota = lax.broadcasted_iota(jnp.int32, s.shape, 0)
    vals = []
    for _ in range(count):
        m = jnp.max(s, axis=0, keepdims=True)
        vals.append(m)
        first = jnp.min(jnp.where(s == m, iota, rows), axis=0, keepdims=True)
        s = jnp.where(iota == first, -jnp.inf, s)
    return jnp.concatenate(vals, axis=0)


def _peer_kernel(h_ref, wqt_ref, keys_ref, u_ref, vt_ref, g2_ref, b2_ref, out_ref,
                 qt_scr, s1_scr, e1_scr, s2_scr, e2_scr, tau_scr, a_scr, acc_scr, *, nc):
    ci = pl.program_id(1)
    tm = h_ref.shape[0]
    g = PEER_GROUP
    nkeys = keys_ref.shape[1]
    hb = h_ref[...].astype(BF16)

    @pl.when(ci == 0)
    def _():
        qt_scr[...] = lax.dot_general(wqt_ref[...], hb, _NT, preferred_element_type=F32)

        def per_head(hd, carry):
            base = pl.multiple_of(hd * (2 * PEER_HALF), 2 * PEER_HALF)
            q1 = qt_scr[pl.ds(base, PEER_HALF), :]
            q2 = qt_scr[pl.ds(base + PEER_HALF, PEER_HALF), :]
            s1 = jnp.dot(keys_ref[0], q1, preferred_element_type=F32)
            s2 = jnp.dot(keys_ref[1], q2, preferred_element_type=F32)
            t1 = _top_values(s1, PEER_TOPK)
            t2 = _top_values(s2, PEER_TOPK)
            cand = jnp.concatenate([t1[a:a + 1, :] + t2 for a in range(PEER_TOPK)], axis=0)
            cv = _top_values(cand, PEER_TOPK)
            zsum = jnp.sum(jnp.exp(cv - cv[0:1, :]), axis=0, keepdims=True)
            s1_scr[hd] = s1
            e1_scr[hd] = jnp.exp(s1 - t1[0:1, :]) / zsum
            s2_scr[hd] = s2
            e2_scr[hd] = jnp.exp(s2 - t2[0:1, :])
            tau_scr[hd] = jnp.broadcast_to(cv[PEER_TOPK - 1:PEER_TOPK, :], (SUBLANES, tm))
            return carry

        lax.fori_loop(0, PEER_HEADS, per_head, 0)
        acc_scr[...] = jnp.zeros_like(acc_scr)

    pre = lax.dot_general(u_ref[...], hb, _NT, preferred_element_type=F32)
    act = 0.5 * pre * (1.0 + lax.erf(pre * (2.0 ** -0.5)))
    for gi in range(g):
        i1 = ci * g + gi
        w = jnp.zeros((nkeys, tm), F32)
        for hd in range(PEER_HEADS):
            s1r = s1_scr[hd, pl.ds(i1, 1), :]
            e1r = e1_scr[hd, pl.ds(i1, 1), :]
            tau = tau_scr[hd, 0:1, :]
            w = w + jnp.where(s2_scr[hd] + s1r >= tau, e2_scr[hd] * e1r, 0.0)
        a_scr[gi * nkeys:(gi + 1) * nkeys, :] = (act[gi * nkeys:(gi + 1) * nkeys, :] * w).astype(BF16)
    acc_scr[...] += jnp.dot(vt_ref[...], a_scr[...], preferred_element_type=F32)

    @pl.when(ci == nc - 1)
    def _():
        y = acc_scr[...].T
        out_ref[...] = _layer_norm(DN_ALPHA * h_ref[...] + y, g2_ref[...], b2_ref[...])


def _peer_call(h, wqt, keys, u, vt, g2, b2):
    n, d = h.shape
    nkeys = keys.shape[1]
    tm = min(_row_tile(n), 256)
    g = PEER_GROUP
    nc = nkeys // g
    hspec = pl.BlockSpec((tm, d), lambda i, c: (i, 0))
    full = lambda a: pl.BlockSpec(a.shape, lambda i, c: (0,) * a.ndim)
    dense = pltpu.VMEM((PEER_HEADS, nkeys, tm), F32)
    return pl.pallas_call(
        functools.partial(_peer_kernel, nc=nc), grid=(n // tm, nc),
        in_specs=[hspec, full(wqt), full(keys),
                  pl.BlockSpec((g * nkeys, d), lambda i, c: (c, 0)),
                  pl.BlockSpec((d, g * nkeys), lambda i, c: (0, c)), full(g2), full(b2)],
        out_specs=hspec, out_shape=jax.ShapeDtypeStruct((n, d), F32),
        scratch_shapes=[pltpu.VMEM((PEER_HEADS * 2 * PEER_HALF, tm), F32), dense, dense, dense, dense,
                        pltpu.VMEM((PEER_HEADS, SUBLANES, tm), F32),
                        pltpu.VMEM((g * nkeys, tm), BF16), pltpu.VMEM((d, tm), F32)],
        compiler_params=_cparams(("parallel", "arbitrary")), name="peer")(h, wqt, keys, u, vt, g2, b2)


def _round_up(x, m):
    return -(-x // m) * m


def _pad_rows(x, rows):
    return jnp.pad(x, ((0, 0), (0, rows - x.shape[1]), (0, 0)))


def kernel(x_prompt, x_sample, cache_k, cache_v, state_gla, meta, ln_in_g, ln_in_b, w_in, w_gk2, b_gk,
           gla_norm_g, w_pa, w_pb, w_o, ln1_g, ln1_b, peer_wq, peer_subkeys, peer_u, peer_v, ln2_g, ln2_b):
    bp, s_len, d = x_prompt.shape
    bd, t_new, _ = x_sample.shape
    past = cache_k.shape[2]
    n_len = N_META + s_len
    keep = min(s_len, past)
    lp = _round_up(n_len, ATTN_BLOCK)
    ts = _round_up(t_new, ATTN_BLOCK)
    tg = _round_up(t_new, GLA_CHUNK)
    tk_s = _round_up(N_META + past + t_new, ATTN_BLOCK)
    row = lambda a: a.reshape(1, -1)

    meta_rows = jnp.broadcast_to(meta[None].astype(x_prompt.dtype), (bp, N_META, d))
    xp = _pad_rows(jnp.concatenate([meta_rows, x_prompt], axis=1), lp).reshape(bp * lp, d)
    xs = _pad_rows(x_sample, ts).reshape(bd * ts, d)
    hp = _ln_call(xp, ln_in_g, ln_in_b)
    hs = _ln_call(xs, ln_in_g, ln_in_b)

    outs = {k: [] for k in ("nk_p", "nv_p", "st_p", "nk_s", "nv_s", "st_s")}
    for l in range(DEPTH):
        glr0 = _SEC[7]
        w_main = jnp.concatenate([w_in[l][:, :glr0], w_in[l][:, glr0 + GLA_RANK:]], axis=1).astype(BF16)
        w_glr = jnp.pad(w_in[l][:, glr0:glr0 + GLA_RANK], ((0, 0), (0, LANES - GLA_RANK))).astype(BF16)
        w_gk2p = jnp.pad(w_gk2[l], ((0, LANES - GLA_RANK), (0, 0)))
        merge_w = (row(gla_norm_g[l]), w_pa[l].astype(BF16), w_pb[l].astype(BF16), w_o[l].astype(BF16),
                   row(ln1_g[l]), row(ln1_b[l]))
        peer_w = (peer_wq[l].T.astype(BF16), peer_subkeys[l], peer_u[l].astype(BF16),
                  peer_v[l].T.astype(BF16), row(ln2_g[l]), row(ln2_b[l]))

        qa, kab, vab, ka, va, qb, kb, vb, la, rb, ga, gb = _inproj_call(hp, w_main, w_glr, w_gk2p, row(b_gk[l]))
        b3 = lambda a: a.reshape(bp, lp, a.shape[-1])
        sb = _sb_attn_call(b3(qa), b3(kab), b3(vab), 0)
        s0t = jnp.zeros((bp, GLA_HEADS, GLA_DV, GLA_DK), F32)
        go, stp = _gla_call(b3(qb), b3(kb), b3(la), b3(vb), s0t, n_len)
        hp = _merge_call(hp, sb.reshape(bp * lp, SB_W), go.reshape(bp * lp, GLA_VW), rb, ga, gb, *merge_w)
        hp = _peer_call(hp, *peer_w)
        ka3 = ka.reshape(bp, lp, SB_HEADS, SB_DIM)
        va3 = va.reshape(bp, lp, SB_HEADS, SB_DIM)
        outs["nk_p"].append(ka3[:, n_len - keep:n_len])
        outs["nv_p"].append(va3[:, n_len - keep:n_len])
        outs["st_p"].append(jnp.swapaxes(stp, 2, 3).astype(state_gla.dtype))

        qa2, _, _, ka2, va2, qb2, kb2, vb2, la2, rb2, ga2, gb2 = _inproj_call(
            hs, w_main, w_glr, w_gk2p, row(b_gk[l]))
        s3 = lambda a: a.reshape(bd, ts, a.shape[-1])
        ka2n = s3(ka2)[:, :t_new]
        va2n = s3(va2)[:, :t_new]
        meta_k = jnp.broadcast_to(ka3[:1, :N_META].reshape(1, N_META, SB_W), (bd, N_META, SB_W))
        meta_v = jnp.broadcast_to(va3[:1, :N_META].reshape(1, N_META, SB_W), (bd, N_META, SB_W))
        k_all = jnp.concatenate([meta_k, cache_k[l].reshape(bd, past, SB_W), ka2n], axis=1)
        v_all = jnp.concatenate([meta_v, cache_v[l].reshape(bd, past, SB_W), va2n], axis=1)
        sb2 = _sb_attn_call(s3(qa2), _pad_rows(k_all, tk_s).astype(BF16), _pad_rows(v_all, tk_s).astype(BF16),
                            N_META + past)
        g3 = lambda a: _pad_rows(s3(a)[:, :t_new], tg)
        go2, sts = _gla_call(g3(qb2), g3(kb2), g3(la2), g3(vb2), jnp.swapaxes(state_gla[l], 2, 3), t_new)
        go2 = _pad_rows(go2[:, :t_new], ts).reshape(bd * ts, GLA_VW)
        hs = _merge_call(hs, sb2.reshape(bd * ts, SB_W), go2, rb2, ga2, gb2, *merge_w)
        hs = _peer_call(hs, *peer_w)
        outs["nk_s"].append(ka2n.reshape(bd, t_new, SB_HEADS, SB_DIM))
        outs["nv_s"].append(va2n.reshape(bd, t_new, SB_HEADS, SB_DIM))
        outs["st_s"].append(jnp.swapaxes(sts, 2, 3).astype(state_gla.dtype))

    y_prompt = hp.reshape(bp, lp, d)[:, N_META:n_len]
    y_sample = hs.reshape(bd, ts, d)[:, :t_new]
    return (y_prompt, y_sample, jnp.stack(outs["nk_p"]), jnp.stack(outs["nv_p"]), jnp.stack(outs["st_p"]),
            jnp.stack(outs["nk_s"]), jnp.stack(outs["nv_s"]), jnp.stack(outs["st_s"]))
```

```python
import functools

import jax
import jax.numpy as jnp
from jax import lax
from jax.experimental import pallas as pl
from jax.experimental.pallas import tpu as pltpu

F32 = jnp.float32
BF16 = jnp.bfloat16

D_MODEL = 1024
DEPTH = 2
N_META = 16
SB_HEADS = 8
SB_DIM = 64
SB_W = SB_HEADS * SB_DIM
GLA_HEADS = 4
GLA_DK = 128
GLA_DV = 256
GLA_KW = GLA_HEADS * GLA_DK
GLA_VW = GLA_HEADS * GLA_DV
GLA_RANK = 16
GLA_TAU = 16.0
PEER_HEADS = 8
PEER_HALF = 128
PEER_TOPK = 16
DN_ALPHA = float((2 * DEPTH) ** 0.25)
LN_EPS = 1e-5
RMS_EPS = 1e-6

LANES = 128
SUBLANES = 8
VMEM_LIMIT_BYTES = 56 * 1024 * 1024

ATTN_BLOCK = 128
GLA_CHUNK = 128
PEER_GROUP = 16
EXP_ZERO_BELOW = -104.0

_NT = (((1,), (1,)), ((), ()))
_TN = (((0,), (0,)), ((), ()))


def _cparams(sem, flags=None):
    return pltpu.CompilerParams(dimension_semantics=sem, vmem_limit_bytes=VMEM_LIMIT_BYTES, flags=flags)


def _row_tile(n):
    for t in (512, 256, 128):
        if n % t == 0:
            return t
    raise ValueError(f"token count {n} is not a multiple of 128")


def _layer_norm(x, g, b):
    mu = jnp.mean(x, axis=-1, keepdims=True)
    xc = x - mu
    var = jnp.mean(xc * xc, axis=-1, keepdims=True)
    return xc * lax.rsqrt(var + LN_EPS) * g + b


def _softplus(x):
    return jnp.maximum(x, 0.0) + jnp.log(1.0 + jnp.exp(-jnp.abs(x)))


def _sigmoid(x):
    return 1.0 / (1.0 + jnp.exp(-x))


def _ln_kernel(x_ref, g_ref, b_ref, o_ref):
    o_ref[...] = _layer_norm(x_ref[...], g_ref[...], b_ref[...])


def _ln_call(x, g, b):
    n, d = x.shape
    tm = _row_tile(n)
    row = pl.BlockSpec((tm, d), lambda i: (i, 0))
    vec = pl.BlockSpec((1, d), lambda i: (0, 0))
    return pl.pallas_call(
        _ln_kernel, grid=(n // tm,), in_specs=[row, vec, vec], out_specs=row,
        out_shape=jax.ShapeDtypeStruct((n, d), F32), compiler_params=_cparams(("parallel",)),
        name="ln_in")(x, g.reshape(1, d), b.reshape(1, d))


_SEC = (0, 512, 1024, 1536, 2048, 2560, 3584, 4608, 5632, 6656)


def _inproj_kernel(h_ref, w_ref, wg_ref, wgk2_ref, bgk_ref,
                   qa_ref, kab_ref, vab_ref, ka_ref, va_ref, qb_ref, kb_ref, vb_ref, la_ref,
                   rb_ref, ga_ref, gb_ref):
    hb = h_ref[...].astype(BF16)

    def sec(i):
        return jnp.dot(hb, w_ref[:, _SEC[i]:_SEC[i + 1]], preferred_element_type=F32)

    qa_ref[...] = (sec(0) * (SB_DIM ** -0.5)).astype(BF16)
    ka = sec(1)
    ka_ref[...] = ka
    kab_ref[...] = ka.astype(BF16)
    va = sec(2)
    va_ref[...] = va
    vab_ref[...] = va.astype(BF16)
    qb_ref[...] = sec(3) * (GLA_DK ** -0.5)
    kb_ref[...] = sec(4)
    vb_ref[...] = sec(5).astype(BF16)
    rb_ref[...] = sec(6)
    ga_ref[...] = sec(7)
    gb_ref[...] = sec(8)
    glr = jnp.dot(hb, wg_ref[...], preferred_element_type=F32)
    x = jnp.dot(glr, wgk2_ref[...], preferred_element_type=F32) + bgk_ref[...]
    la_ref[...] = -_softplus(-x) * (1.0 / GLA_TAU)


def _inproj_call(h, w_main, w_glr, w_gk2p, b_gk):
    n, d = h.shape
    tm = min(_row_tile(n), 256)
    row = lambda w: pl.BlockSpec((tm, w), lambda i: (i, 0))
    full = lambda a: pl.BlockSpec(a.shape, lambda i: (0,) * a.ndim)
    outs = [(SB_W, BF16), (SB_W, BF16), (SB_W, BF16), (SB_W, F32), (SB_W, F32),
            (GLA_KW, F32), (GLA_KW, F32), (GLA_VW, BF16), (GLA_KW, F32),
            (GLA_VW, F32), (D_MODEL, F32), (D_MODEL, F32)]
    return pl.pallas_call(
        _inproj_kernel, grid=(n // tm,),
        in_specs=[row(d), full(w_main), full(w_glr), full(w_gk2p), full(b_gk)],
        out_specs=[row(w) for w, _ in outs],
        out_shape=[jax.ShapeDtypeStruct((n, w), t) for w, t in outs],
        compiler_params=_cparams(("parallel",)), name="in_proj")(h, w_main, w_glr, w_gk2p, b_gk)


def _sb_attn_kernel(q_ref, k_ref, v_ref, o_ref, carry_scr, acc_scr, *, q_off, nk):
    tq = tk = ATTN_BLOCK
    npair = SB_W // LANES
    i = pl.program_id(1)
    lane = lax.broadcasted_iota(jnp.int32, (tq, LANES), 1)
    r_iota = lax.broadcasted_iota(jnp.int32, (tq, tk), 0)
    c_iota = lax.broadcasted_iota(jnp.int32, (tq, tk), 1)
    later_ones = jnp.concatenate([(r_iota > c_iota).astype(BF16), jnp.ones((tk, tk), BF16)], axis=1)
    qpos = q_off + i * tq + r_iota
    j_start = jnp.minimum((q_off + i * tq + tq - 2) // tk, nk - 1)
    carry_scr[...] = jnp.zeros_like(carry_scr)
    acc_scr[...] = jnp.zeros_like(acc_scr)

    def cond(c):
        j, cmax = c
        return jnp.logical_and(j >= 0, cmax > EXP_ZERO_BELOW)

    def body(c):
        j, _ = c
        off = pl.multiple_of(j * tk, tk)
        mask = (off + c_iota) < qpos
        heads = range(SB_HEADS)
        zs = []
        for h in heads:
            cols = slice((h // 2) * LANES, (h // 2 + 1) * LANES)
            q = q_ref[0, :, cols]
            qm = jnp.where(lane < SB_DIM if h % 2 == 0 else lane >= SB_DIM, q, jnp.zeros_like(q))
            zs.append(lax.dot_general(qm, k_ref[0, pl.ds(off, tk), cols], _NT, preferred_element_type=F32))
        lbs, css = [], []
        for h in heads:
            sp = _softplus(zs[h])
            lbs.append(zs[h] - sp)
            lk = jnp.where(mask, -sp, 0.0)
            hi = lk.astype(BF16)
            lo = (lk - hi.astype(F32)).astype(BF16)
            css.append(jnp.dot(hi, later_ones, preferred_element_type=F32)
                       + jnp.dot(lo, later_ones, preferred_element_type=F32))
        cmax = None
        pvs = []
        for h in heads:
            cols = slice((h // 2) * LANES, (h // 2 + 1) * LANES)
            carry = carry_scr[h]
            w = jnp.where(mask, jnp.exp(lbs[h] + css[h][:, :tk] + carry), 0.0).astype(BF16)
            pvs.append(jnp.dot(w, v_ref[0, pl.ds(off, tk), cols], preferred_element_type=F32))
            carry = carry + css[h][:, tk:]
            carry_scr[h] = carry
            cmax = carry if cmax is None else jnp.maximum(cmax, carry)
        for p in range(npair):
            acc_scr[p] += jnp.where(lane < SB_DIM, pvs[2 * p], pvs[2 * p + 1])
        return j - 1, jnp.max(cmax)

    lax.while_loop(cond, body, (j_start, jnp.float32(0.0)))
    for p in range(npair):
        o_ref[0, :, p * LANES:(p + 1) * LANES] = acc_scr[p]


def _sb_attn_call(q, k, v, q_off):
    b, tq_all, _ = q.shape
    tk_all = k.shape[1]
    nq, nk = tq_all // ATTN_BLOCK, tk_all // ATTN_BLOCK
    qspec = pl.BlockSpec((1, ATTN_BLOCK, SB_W), lambda bi, i: (bi, i, 0))
    kspec = pl.BlockSpec((1, tk_all, SB_W), lambda bi, i: (bi, 0, 0))
    return pl.pallas_call(
        functools.partial(_sb_attn_kernel, q_off=q_off, nk=nk),
        grid=(b, nq), in_specs=[qspec, kspec, kspec], out_specs=qspec,
        out_shape=jax.ShapeDtypeStruct((b, tq_all, SB_W), F32),
        scratch_shapes=[pltpu.VMEM((SB_HEADS, ATTN_BLOCK, ATTN_BLOCK), F32),
                        pltpu.VMEM((SB_W // LANES, ATTN_BLOCK, LANES), F32)],
        compiler_params=_cparams(("parallel", "arbitrary")), name="sb_attn")(q, k, v)


def _half_sizes(c):
    out, m = [], c // 2
    while m >= 1:
        out.append(m)
        m //= 2
    return out


def _block_ref_rows(b, m):
    c, dk = b.shape
    if 2 * m >= SUBLANES:
        pieces = [jnp.broadcast_to(b[a + m - 1:a + m, :], (2 * m, dk)) for a in range(0, c, 2 * m)]
        return pieces[0] if len(pieces) == 1 else jnp.concatenate(pieces, axis=0)
    row8 = lax.broadcasted_iota(jnp.int32, (SUBLANES, dk), 0)
    tiles = []
    for a in range(0, c, SUBLANES):
        tile = None
        for off in range(0, SUBLANES, 2 * m):
            r = jnp.broadcast_to(b[a + off + m - 1:a + off + m, :], (SUBLANES, dk))
            tile = r if tile is None else jnp.where(row8 >= off, r, tile)
        tiles.append(tile)
    return jnp.concatenate(tiles, axis=0)


def _gla_kernel(q_ref, k_ref, la_ref, v_ref, s0_ref, o_ref, st_ref, *, n_valid):
    c = GLA_CHUNK
    ci = pl.program_id(1)

    @pl.when(ci == 0)
    def _():
        st_ref[...] = s0_ref[...]

    r_iota = lax.broadcasted_iota(jnp.int32, (c, c), 0)
    c_iota = lax.broadcasted_iota(jnp.int32, (c, c), 1)
    tril = (r_iota >= c_iota).astype(BF16)
    level = jnp.where(c_iota < r_iota, 31 - lax.clz(r_iota ^ c_iota), jnp.where(c_iota == r_iota, -1, -2))
    valid = (ci * c + lax.broadcasted_iota(jnp.int32, (c, GLA_DK), 0)) < n_valid

    for h in range(GLA_HEADS):
        ks = slice(h * GLA_DK, (h + 1) * GLA_DK)
        vs = slice(h * GLA_DV, (h + 1) * GLA_DV)
        la = jnp.where(valid, la_ref[0, :, ks], 0.0)
        k = jnp.where(valid, k_ref[0, :, ks], 0.0)
        q = q_ref[0, :, ks]
        v = v_ref[0, :, vs]
        l1 = la.astype(BF16)
        r1 = la - l1.astype(F32)
        l2 = r1.astype(BF16)
        l3 = (r1 - l2.astype(F32)).astype(BF16)
        b = (jnp.dot(tril, l1, preferred_element_type=F32) + jnp.dot(tril, l2, preferred_element_type=F32)
             + jnp.dot(tril, l3, preferred_element_type=F32))
        b_last = b[c - 1:c, :]
        st = st_ref[0, h]
        inter = lax.dot_general((q * jnp.exp(b)).astype(BF16), st.astype(BF16), _NT,
                                preferred_element_type=F32)
        scores = jnp.where(level == -1, lax.dot_general(q.astype(BF16), k.astype(BF16), _NT,
                                                        preferred_element_type=F32), 0.0)
        for m in _half_sizes(c):
            ref_rows = _block_ref_rows(b, m)
            qm = (q * jnp.exp(jnp.minimum(b - ref_rows, 0.0))).astype(BF16)
            km = (k * jnp.exp(jnp.minimum(ref_rows - b, 0.0))).astype(BF16)
            scores = jnp.where(level == m.bit_length() - 1,
                               lax.dot_general(qm, km, _NT, preferred_element_type=F32), scores)
        intra = jnp.dot(scores.astype(BF16), v, preferred_element_type=F32)
        o_ref[0, :, vs] = inter + intra
        kd = (k * jnp.exp(b_last - b)).astype(BF16)
        st_ref[0, h] = st * jnp.exp(b_last) + lax.dot_general(v, kd, _TN, preferred_element_type=F32)


def _gla_call(q, k, la, v, s0t, n_valid):
    b, t, _ = q.shape
    c = GLA_CHUNK
    kspec = pl.BlockSpec((1, c, GLA_KW), lambda bi, ci: (bi, ci, 0))
    vspec = pl.BlockSpec((1, c, GLA_VW), lambda bi, ci: (bi, ci, 0))
    sspec = pl.BlockSpec((1, GLA_HEADS, GLA_DV, GLA_DK), lambda bi, ci: (bi, 0, 0, 0))
    return pl.pallas_call(
        functools.partial(_gla_kernel, n_valid=n_valid),
        grid=(b, t // c), in_specs=[kspec, kspec, kspec, vspec, sspec],
        out_specs=[vspec, sspec],
        out_shape=[jax.ShapeDtypeStruct((b, t, GLA_VW), F32),
                   jax.ShapeDtypeStruct((b, GLA_HEADS, GLA_DV, GLA_DK), F32)],
        compiler_params=_cparams(("parallel", "arbitrary")), name="gla")(q, k, la, v, s0t)


def _merge_kernel(h_ref, sb_ref, go_ref, rb_ref, ga_ref, gb_ref, gn_ref, wpa_ref, wpb_ref, wo_ref,
                  g1_ref, b1_ref, out_ref):
    go = go_ref[...]
    parts = []
    for h in range(GLA_HEADS):
        x = go[:, h * GLA_DV:(h + 1) * GLA_DV]
        ms = jnp.mean(x * x, axis=-1, keepdims=True)
        parts.append(x * lax.rsqrt(ms + RMS_EPS) * gn_ref[...])
    o = jnp.concatenate(parts, axis=-1)
    rb = rb_ref[...]
    gated = (o * (rb * _sigmoid(rb))).astype(BF16)
    gla_y = jnp.dot(gated, wpb_ref[...], preferred_element_type=F32)
    sb_y = jnp.dot(sb_ref[...].astype(BF16), wpa_ref[...], preferred_element_type=F32)
    merged = _sigmoid(ga_ref[...]) * sb_y + _sigmoid(gb_ref[...]) * gla_y
    y = jnp.dot(merged.astype(BF16), wo_ref[...], preferred_element_type=F32)
    out_ref[...] = _layer_norm(DN_ALPHA * h_ref[...] + y, g1_ref[...], b1_ref[...])


def _merge_call(h, sb, go, rb, ga, gb, gn, wpa, wpb, wo, g1, b1):
    n, d = h.shape
    tm = min(_row_tile(n), 256)
    row = lambda w: pl.BlockSpec((tm, w), lambda i: (i, 0))
    full = lambda a: pl.BlockSpec(a.shape, lambda i: (0,) * a.ndim)
    args = (h, sb, go, rb, ga, gb, gn, wpa, wpb, wo, g1, b1)
    return pl.pallas_call(
        _merge_kernel, grid=(n // tm,),
        in_specs=[row(d), row(SB_W), row(GLA_VW), row(GLA_VW), row(d), row(d)] + [full(a) for a in args[6:]],
        out_specs=row(d), out_shape=jax.ShapeDtypeStruct((n, d), F32),
        compiler_params=_cparams(("parallel",)), name="merge")(*args)


def _sort_network(n):
    def merge(lo, hi, r):
        step = r * 2
        if step < hi - lo:
            yield from merge(lo, hi, step)
            yield from merge(lo + r, hi, step)
            yield from [(i, i + r) for i in range(lo + r, hi - r, step)]
        else:
            yield (lo, lo + r)

    def sort(lo, hi):
        if hi - lo >= 1:
            mid = lo + (hi - lo) // 2
            yield from sort(lo, mid)
            yield from sort(mid + 1, hi)
            yield from merge(lo, hi, 1)

    return tuple(sort(0, n - 1))


def _bitonic_merge_network(n):
    out, k = [], n // 2
    while k >= 1:
        out.extend((i, i + k) for i in range(n) if not i & k)
        k //= 2
    return tuple(out)


_SORT_TOPK = _sort_network(PEER_TOPK)
_MERGE_TOPK = _bitonic_merge_network(PEER_TOPK)


def _exchange(v, net):
    v = list(v)
    for i, j in net:
        hi, lo = jnp.maximum(v[i], v[j]), jnp.minimum(v[i], v[j])
        v[i], v[j] = hi, lo
    return v


def _merge_top(x, y):
    n = PEER_TOPK
    return _exchange([jnp.maximum(x[k], y[n - 1 - k]) for k in range(n)], _MERGE_TOPK)


def _column_top(s):
    v = _exchange([s[SUBLANES * k:SUBLANES * (k + 1), :] for k in range(PEER_TOPK)], _SORT_TOPK)
    for shift in (4, 2, 1):
        v = _merge_top(v, [pltpu.roll(x, shift, 0) for x in v])
    return v


_CAND_COLS = tuple(PEER_TOPK // (a + 1) for a in range(PEER_TOPK // 2))


def _peer_kernel(h_ref, wqt_ref, keys_ref, u_ref, vt_ref, g2_ref, b2_ref, out_ref,
                 qt_scr, s_scr, e1_scr, e2_scr, top_scr, stat_scr, acc_scr, *, nc):
    ci = pl.program_id(1)
    tm = h_ref.shape[0]
    nkeys = keys_ref.shape[1]
    hb = h_ref[...].astype(BF16)

    @pl.when(ci == 0)
    def _():
        qt_scr[...] = lax.dot_general(wqt_ref[...], hb, _NT, preferred_element_type=F32)
        top_scr[...] = jnp.zeros_like(top_scr)
        sub = lax.broadcasted_iota(jnp.int32, (SUBLANES, tm), 0)

        def per_half(hs, carry):
            hd = lax.shift_right_logical(hs, 1)
            side = hs & 1
            qh = qt_scr[pl.ds(pl.multiple_of(hs * PEER_HALF, PEER_HALF), PEER_HALF), :]
            s = jnp.dot(keys_ref[side], qh, preferred_element_type=F32)
            s_scr[hs] = s
            top = _column_top(s)
            for k in range(PEER_TOPK):
                top_scr[side, k] = jnp.where(sub == hd, top[k], top_scr[side, k])
            return carry

        lax.fori_loop(0, 2 * PEER_HEADS, per_half, 0)

        t1 = [top_scr[0, k] for k in range(PEER_TOPK)]
        t2 = [top_scr[1, k] for k in range(PEER_TOPK)]
        neg = jnp.full((SUBLANES, tm), -jnp.inf, F32)
        lists = [[t1[a] + t2[b] for b in range(nb)] + [neg] * (PEER_TOPK - nb) for a, nb in enumerate(_CAND_COLS)]
        half = PEER_TOPK // 2
        lists.append([t1[a] + t2[0] for a in range(half, PEER_TOPK)] + [neg] * half)
        while len(lists) > 1:
            lists = [_merge_top(lists[i], lists[i + 1]) if i + 1 < len(lists) else lists[i]
                     for i in range(0, len(lists), 2)]
        cv = lists[0]
        zsum = jnp.exp(cv[0] - cv[0])
        for k in range(1, PEER_TOPK):
            zsum = zsum + jnp.exp(cv[k] - cv[0])
        stat_scr[0] = cv[PEER_TOPK - 1]
        stat_scr[1] = zsum
        stat_scr[2] = t1[0]
        stat_scr[3] = t2[0]

        def per_head(hd, carry):
            zr = stat_scr[1, pl.ds(hd, 1), :]
            m1 = stat_scr[2, pl.ds(hd, 1), :]
            m2 = stat_scr[3, pl.ds(hd, 1), :]
            e1_scr[hd] = jnp.exp(s_scr[2 * hd] - m1) / zr
            e2_scr[hd] = jnp.exp(s_scr[2 * hd + 1] - m2)
            return carry

        lax.fori_loop(0, PEER_HEADS, per_head, 0)
        acc_scr[...] = jnp.zeros_like(acc_scr)

    piece = 2 * nkeys
    npiece = PEER_GROUP // 2

    def pre_piece(pi):
        return lax.dot_general(u_ref[pi * piece:(pi + 1) * piece, :], hb, _NT, preferred_element_type=F32)

    def gate_rows(i1):
        w = jnp.zeros((nkeys, tm), F32)
        for hd in range(PEER_HEADS):
            s1r = s_scr[2 * hd, pl.ds(i1, 1), :]
            e1r = e1_scr[hd, pl.ds(i1, 1), :]
            tau = stat_scr[0, hd:hd + 1, :]
            w = w + jnp.where(s_scr[2 * hd + 1] + s1r >= tau, e2_scr[hd] * e1r, 0.0)
        return w

    for pi in range(npiece):
        pre = pre_piece(pi)
        i1 = ci * PEER_GROUP + 2 * pi
        w = jnp.concatenate([gate_rows(i1), gate_rows(i1 + 1)], axis=0)
        act = 0.5 * pre * (1.0 + lax.erf(pre * (2.0 ** -0.5)))
        a = (act * w).astype(BF16)
        acc_scr[...] += jnp.dot(vt_ref[:, pi * piece:(pi + 1) * piece], a, preferred_element_type=F32)

    @pl.when(ci == nc - 1)
    def _():
        y = acc_scr[...].T
        out_ref[...] = _layer_norm(DN_ALPHA * h_ref[...] + y, g2_ref[...], b2_ref[...])


def _peer_call(h, wqt, keys, u, vt, g2, b2):
    n, d = h.shape
    nkeys = keys.shape[1]
    tm = min(_row_tile(n), 256)
    g = PEER_GROUP
    nc = nkeys // g
    hspec = pl.BlockSpec((tm, d), lambda i, c: (i, 0))
    full = lambda a: pl.BlockSpec(a.shape, lambda i, c: (0,) * a.ndim)
    assert nkeys == PEER_TOPK * SUBLANES and PEER_HEADS == SUBLANES
    dense = pltpu.VMEM((PEER_HEADS, nkeys, tm), F32)
    return pl.pallas_call(
        functools.partial(_peer_kernel, nc=nc), grid=(n // tm, nc),
        in_specs=[hspec, full(wqt), full(keys),
                  pl.BlockSpec((g * nkeys, d), lambda i, c: (c, 0)),
                  pl.BlockSpec((d, g * nkeys), lambda i, c: (0, c)), full(g2), full(b2)],
        out_specs=hspec, out_shape=jax.ShapeDtypeStruct((n, d), F32),
        scratch_shapes=[pltpu.VMEM((PEER_HEADS * 2 * PEER_HALF, tm), F32),
                        pltpu.VMEM((2 * PEER_HEADS, nkeys, tm), F32), dense, dense,
                        pltpu.VMEM((2, PEER_TOPK, SUBLANES, tm), F32), pltpu.VMEM((4, SUBLANES, tm), F32),
                        pltpu.VMEM((d, tm), F32)],
        compiler_params=_cparams(("parallel", "arbitrary")), name="peer")(h, wqt, keys, u, vt, g2, b2)


def _round_up(x, m):
    return -(-x // m) * m


def _pad_rows(x, rows):
    return jnp.pad(x, ((0, 0), (0, rows - x.shape[1]), (0, 0)))


def kernel(x_prompt, x_sample, cache_k, cache_v, state_gla, meta, ln_in_g, ln_in_b, w_in, w_gk2, b_gk,
           gla_norm_g, w_pa, w_pb, w_o, ln1_g, ln1_b, peer_wq, peer_subkeys, peer_u, peer_v, ln2_g, ln2_b):
    bp, s_len, d = x_prompt.shape
    bd, t_new, _ = x_sample.shape
    past = cache_k.shape[2]
    n_len = N_META + s_len
    keep = min(s_len, past)
    lp = _round_up(n_len, ATTN_BLOCK)
    ts = _round_up(t_new, ATTN_BLOCK)
    tg = _round_up(t_new, GLA_CHUNK)
    tk_s = _round_up(N_META + past + t_new, ATTN_BLOCK)
    row = lambda a: a.reshape(1, -1)

    meta_rows = jnp.broadcast_to(meta[None].astype(x_prompt.dtype), (bp, N_META, d))
    xp = _pad_rows(jnp.concatenate([meta_rows, x_prompt], axis=1), lp).reshape(bp * lp, d)
    xs = _pad_rows(x_sample, ts).reshape(bd * ts, d)
    hp = _ln_call(xp, ln_in_g, ln_in_b)
    hs = _ln_call(xs, ln_in_g, ln_in_b)

    outs = {k: [] for k in ("nk_p", "nv_p", "st_p", "nk_s", "nv_s", "st_s")}
    for l in range(DEPTH):
        glr0 = _SEC[7]
        w_main = jnp.concatenate([w_in[l][:, :glr0], w_in[l][:, glr0 + GLA_RANK:]], axis=1).astype(BF16)
        w_glr = jnp.pad(w_in[l][:, glr0:glr0 + GLA_RANK], ((0, 0), (0, LANES - GLA_RANK))).astype(BF16)
        w_gk2p = jnp.pad(w_gk2[l], ((0, LANES - GLA_RANK), (0, 0)))
        merge_w = (row(gla_norm_g[l]), w_pa[l].astype(BF16), w_pb[l].astype(BF16), w_o[l].astype(BF16),
                   row(ln1_g[l]), row(ln1_b[l]))
        peer_w = (peer_wq[l].T.astype(BF16), peer_subkeys[l], peer_u[l].astype(BF16),
                  peer_v[l].T.astype(BF16), row(ln2_g[l]), row(ln2_b[l]))

        qa, kab, vab, ka, va, qb, kb, vb, la, rb, ga, gb = _inproj_call(hp, w_main, w_glr, w_gk2p, row(b_gk[l]))
        b3 = lambda a: a.reshape(bp, lp, a.shape[-1])
        sb = _sb_attn_call(b3(qa), b3(kab), b3(vab), 0)
        s0t = jnp.zeros((bp, GLA_HEADS, GLA_DV, GLA_DK), F32)
        go, stp = _gla_call(b3(qb), b3(kb), b3(la), b3(vb), s0t, n_len)
        hp = _merge_call(hp, sb.reshape(bp * lp, SB_W), go.reshape(bp * lp, GLA_VW), rb, ga, gb, *merge_w)
        hp = _peer_call(hp, *peer_w)
        ka3 = ka.reshape(bp, lp, SB_HEADS, SB_DIM)
        va3 = va.reshape(bp, lp, SB_HEADS, SB_DIM)
        outs["nk_p"].append(ka3[:, n_len - keep:n_len])
        outs["nv_p"].append(va3[:, n_len - keep:n_len])
        outs["st_p"].append(jnp.swapaxes(stp, 2, 3).astype(state_gla.dtype))

        qa2, _, _, ka2, va2, qb2, kb2, vb2, la2, rb2, ga2, gb2 = _inproj_call(
            hs, w_main, w_glr, w_gk2p, row(b_gk[l]))
        s3 = lambda a: a.reshape(bd, ts, a.shape[-1])
        ka2n = s3(ka2)[:, :t_new]
        va2n = s3(va2)[:, :t_new]
        meta_k = jnp.broadcast_to(ka3[:1, :N_META].reshape(1, N_META, SB_W), (bd, N_META, SB_W))
        meta_v = jnp.broadcast_to(va3[:1, :N_META].reshape(1, N_META, SB_W), (bd, N_META, SB_W))
        k_all = jnp.concatenate([meta_k, cache_k[l].reshape(bd, past, SB_W), ka2n], axis=1)
        v_all = jnp.concatenate([meta_v, cache_v[l].reshape(bd, past, SB_W), va2n], axis=1)
        sb2 = _sb_attn_call(s3(qa2), _pad_rows(k_all, tk_s).astype(BF16), _pad_rows(v_all, tk_s).astype(BF16),
                            N_META + past)
        g3 = lambda a: _pad_rows(s3(a)[:, :t_new], tg)
        go2, sts = _gla_call(g3(qb2), g3(kb2), g3(la2), g3(vb2), jnp.swapaxes(state_gla[l], 2, 3), t_new)
        go2 = _pad_rows(go2[:, :t_new], ts).reshape(bd * ts, GLA_VW)
        hs = _merge_call(hs, sb2.reshape(bd * ts, SB_W), go2, rb2, ga2, gb2, *merge_w)
        hs = _peer_call(hs, *peer_w)
        outs["nk_s"].append(ka2n.reshape(bd, t_new, SB_HEADS, SB_DIM))
        outs["nv_s"].append(va2n.reshape(bd, t_new, SB_HEADS, SB_DIM))
        outs["st_s"].append(jnp.swapaxes(sts, 2, 3).astype(state_gla.dtype))

    y_prompt = hp.reshape(bp, lp, d)[:, N_META:n_len]
    y_sample = hs.reshape(bd, ts, d)[:, :t_new]
    return (y_prompt, y_sample, jnp.stack(outs["nk_p"]), jnp.stack(outs["nv_p"]), jnp.stack(outs["st_p"]),
            jnp.stack(outs["nk_s"]), jnp.stack(outs["nv_s"]), jnp.stack(outs["st_s"]))
```

```python
import functools

import jax
import jax.numpy as jnp
from jax import lax
from jax.experimental import pallas as pl
from jax.experimental.pallas import tpu as pltpu

F32 = jnp.float32
BF16 = jnp.bfloat16

D_MODEL = 1024
DEPTH = 2
N_META = 16
SB_HEADS = 8
SB_DIM = 64
SB_W = SB_HEADS * SB_DIM
GLA_HEADS = 4
GLA_DK = 128
GLA_DV = 256
GLA_KW = GLA_HEADS * GLA_DK
GLA_VW = GLA_HEADS * GLA_DV
GLA_RANK = 16
GLA_TAU = 16.0
PEER_HEADS = 8
PEER_HALF = 128
PEER_TOPK = 16
DN_ALPHA = float((2 * DEPTH) ** 0.25)
LN_EPS = 1e-5
RMS_EPS = 1e-6

LANES = 128
SUBLANES = 8
VMEM_LIMIT_BYTES = 56 * 1024 * 1024

ATTN_BLOCK = 128
GLA_CHUNK = 128
PEER_GROUP = 32
EXP_ZERO_BELOW = -104.0

_NT = (((1,), (1,)), ((), ()))
_TN = (((0,), (0,)), ((), ()))


def _cparams(sem, flags=None):
    return pltpu.CompilerParams(dimension_semantics=sem, vmem_limit_bytes=VMEM_LIMIT_BYTES, flags=flags)


def _row_tile(n):
    for t in (512, 256, 128):
        if n % t == 0:
            return t
    raise ValueError(f"token count {n} is not a multiple of 128")


def _layer_norm(x, g, b):
    mu = jnp.mean(x, axis=-1, keepdims=True)
    xc = x - mu
    var = jnp.mean(xc * xc, axis=-1, keepdims=True)
    return xc * lax.rsqrt(var + LN_EPS) * g + b


def _softplus(x):
    return jnp.maximum(x, 0.0) + jnp.log(1.0 + jnp.exp(-jnp.abs(x)))


def _sigmoid(x):
    return 1.0 / (1.0 + jnp.exp(-x))


def _ln_kernel(x_ref, g_ref, b_ref, o_ref):
    o_ref[...] = _layer_norm(x_ref[...], g_ref[...], b_ref[...])


def _ln_call(x, g, b):
    n, d = x.shape
    tm = _row_tile(n)
    row = pl.BlockSpec((tm, d), lambda i: (i, 0))
    vec = pl.BlockSpec((1, d), lambda i: (0, 0))
    return pl.pallas_call(
        _ln_kernel, grid=(n // tm,), in_specs=[row, vec, vec], out_specs=row,
        out_shape=jax.ShapeDtypeStruct((n, d), F32), compiler_params=_cparams(("parallel",)),
        name="ln_in")(x, g.reshape(1, d), b.reshape(1, d))


_SEC = (0, 512, 1024, 1536, 2048, 2560, 3584, 4608, 5632, 6656)


def _inproj_kernel(h_ref, w_ref, wg_ref, wgk2_ref, bgk_ref,
                   qa_ref, kab_ref, vab_ref, ka_ref, va_ref, qb_ref, kb_ref, vb_ref, la_ref,
                   rb_ref, ga_ref, gb_ref):
    hb = h_ref[...].astype(BF16)

    def sec(i):
        return jnp.dot(hb, w_ref[:, _SEC[i]:_SEC[i + 1]], preferred_element_type=F32)

    qa_ref[...] = (sec(0) * (SB_DIM ** -0.5)).astype(BF16)
    ka = sec(1)
    ka_ref[...] = ka
    kab_ref[...] = ka.astype(BF16)
    va = sec(2)
    va_ref[...] = va
    vab_ref[...] = va.astype(BF16)
    qb_ref[...] = sec(3) * (GLA_DK ** -0.5)
    kb_ref[...] = sec(4)
    vb_ref[...] = sec(5).astype(BF16)
    rb_ref[...] = sec(6)
    ga_ref[...] = sec(7)
    gb_ref[...] = sec(8)
    glr = jnp.dot(hb, wg_ref[...], preferred_element_type=F32)
    x = jnp.dot(glr, wgk2_ref[...], preferred_element_type=F32) + bgk_ref[...]
    la_ref[...] = -_softplus(-x) * (1.0 / GLA_TAU)


def _inproj_call(h, w_main, w_glr, w_gk2p, b_gk):
    n, d = h.shape
    tm = min(_row_tile(n), 256)
    row = lambda w: pl.BlockSpec((tm, w), lambda i: (i, 0))
    full = lambda a: pl.BlockSpec(a.shape, lambda i: (0,) * a.ndim)
    outs = [(SB_W, BF16), (SB_W, BF16), (SB_W, BF16), (SB_W, F32), (SB_W, F32),
            (GLA_KW, F32), (GLA_KW, F32), (GLA_VW, BF16), (GLA_KW, F32),
            (GLA_VW, F32), (D_MODEL, F32), (D_MODEL, F32)]
    return pl.pallas_call(
        _inproj_kernel, grid=(n // tm,),
        in_specs=[row(d), full(w_main), full(w_glr), full(w_gk2p), full(b_gk)],
        out_specs=[row(w) for w, _ in outs],
        out_shape=[jax.ShapeDtypeStruct((n, w), t) for w, t in outs],
        compiler_params=_cparams(("parallel",)), name="in_proj")(h, w_main, w_glr, w_gk2p, b_gk)


def _sb_attn_kernel(q_ref, k_ref, v_ref, o_ref, carry_scr, acc_scr, *, q_off, nk):
    tq = tk = ATTN_BLOCK
    npair = SB_W // LANES
    i = pl.program_id(1)
    lane = lax.broadcasted_iota(jnp.int32, (tq, LANES), 1)
    r_iota = lax.broadcasted_iota(jnp.int32, (tq, tk), 0)
    c_iota = lax.broadcasted_iota(jnp.int32, (tq, tk), 1)
    later_ones = jnp.concatenate([(r_iota > c_iota).astype(BF16), jnp.ones((tk, tk), BF16)], axis=1)
    qpos = q_off + i * tq + r_iota
    j_start = jnp.minimum((q_off + i * tq + tq - 2) // tk, nk - 1)
    carry_scr[...] = jnp.zeros_like(carry_scr)
    acc_scr[...] = jnp.zeros_like(acc_scr)

    def cond(c):
        j, cmax = c
        return jnp.logical_and(j >= 0, cmax > EXP_ZERO_BELOW)

    def body(c):
        j, _ = c
        off = pl.multiple_of(j * tk, tk)
        mask = (off + c_iota) < qpos
        heads = range(SB_HEADS)
        zs = []
        for h in heads:
            cols = slice((h // 2) * LANES, (h // 2 + 1) * LANES)
            q = q_ref[0, :, cols]
            qm = jnp.where(lane < SB_DIM if h % 2 == 0 else lane >= SB_DIM, q, jnp.zeros_like(q))
            zs.append(lax.dot_general(qm, k_ref[0, pl.ds(off, tk), cols], _NT, preferred_element_type=F32))
        lbs, css = [], []
        for h in heads:
            sp = _softplus(zs[h])
            lbs.append(zs[h] - sp)
            lk = jnp.where(mask, -sp, 0.0)
            hi = lk.astype(BF16)
            lo = (lk - hi.astype(F32)).astype(BF16)
            css.append(jnp.dot(hi, later_ones, preferred_element_type=F32)
                       + jnp.dot(lo, later_ones, preferred_element_type=F32))
        cmax = None
        pvs = []
        for h in heads:
            cols = slice((h // 2) * LANES, (h // 2 + 1) * LANES)
            carry = carry_scr[h]
            w = jnp.where(mask, jnp.exp(lbs[h] + css[h][:, :tk] + carry), 0.0).astype(BF16)
            pvs.append(jnp.dot(w, v_ref[0, pl.ds(off, tk), cols], preferred_element_type=F32))
            carry = carry + css[h][:, tk:]
            carry_scr[h] = carry
            cmax = carry if cmax is None else jnp.maximum(cmax, carry)
        for p in range(npair):
            acc_scr[p] += jnp.where(lane < SB_DIM, pvs[2 * p], pvs[2 * p + 1])
        return j - 1, jnp.max(cmax)

    lax.while_loop(cond, body, (j_start, jnp.float32(0.0)))
    for p in range(npair):
        o_ref[0, :, p * LANES:(p + 1) * LANES] = acc_scr[p]


def _sb_attn_call(q, k, v, q_off):
    b, tq_all, _ = q.shape
    tk_all = k.shape[1]
    nq, nk = tq_all // ATTN_BLOCK, tk_all // ATTN_BLOCK
    qspec = pl.BlockSpec((1, ATTN_BLOCK, SB_W), lambda bi, i: (bi, i, 0))
    kspec = pl.BlockSpec((1, tk_all, SB_W), lambda bi, i: (bi, 0, 0))
    return pl.pallas_call(
        functools.partial(_sb_attn_kernel, q_off=q_off, nk=nk),
        grid=(b, nq), in_specs=[qspec, kspec, kspec], out_specs=qspec,
        out_shape=jax.ShapeDtypeStruct((b, tq_all, SB_W), F32),
        scratch_shapes=[pltpu.VMEM((SB_HEADS, ATTN_BLOCK, ATTN_BLOCK), F32),
                        pltpu.VMEM((SB_W // LANES, ATTN_BLOCK, LANES), F32)],
        compiler_params=_cparams(("parallel", "arbitrary")), name="sb_attn")(q, k, v)


def _half_sizes(c):
    out, m = [], c // 2
    while m >= 1:
        out.append(m)
        m //= 2
    return out


def _block_ref_rows(b, m):
    c, dk = b.shape
    if 2 * m >= SUBLANES:
        pieces = [jnp.broadcast_to(b[a + m - 1:a + m, :], (2 * m, dk)) for a in range(0, c, 2 * m)]
        return pieces[0] if len(pieces) == 1 else jnp.concatenate(pieces, axis=0)
    row8 = lax.broadcasted_iota(jnp.int32, (SUBLANES, dk), 0)
    tiles = []
    for a in range(0, c, SUBLANES):
        tile = None
        for off in range(0, SUBLANES, 2 * m):
            r = jnp.broadcast_to(b[a + off + m - 1:a + off + m, :], (SUBLANES, dk))
            tile = r if tile is None else jnp.where(row8 >= off, r, tile)
        tiles.append(tile)
    return jnp.concatenate(tiles, axis=0)


def _gla_kernel(q_ref, k_ref, la_ref, v_ref, s0_ref, o_ref, st_ref, *, n_valid):
    c = GLA_CHUNK
    ci = pl.program_id(1)

    @pl.when(ci == 0)
    def _():
        st_ref[...] = s0_ref[...]

    r_iota = lax.broadcasted_iota(jnp.int32, (c, c), 0)
    c_iota = lax.broadcasted_iota(jnp.int32, (c, c), 1)
    tril = (r_iota >= c_iota).astype(BF16)
    level = jnp.where(c_iota < r_iota, 31 - lax.clz(r_iota ^ c_iota), jnp.where(c_iota == r_iota, -1, -2))
    valid = (ci * c + lax.broadcasted_iota(jnp.int32, (c, GLA_DK), 0)) < n_valid

    for h in range(GLA_HEADS):
        ks = slice(h * GLA_DK, (h + 1) * GLA_DK)
        vs = slice(h * GLA_DV, (h + 1) * GLA_DV)
        la = jnp.where(valid, la_ref[0, :, ks], 0.0)
        k = jnp.where(valid, k_ref[0, :, ks], 0.0)
        q = q_ref[0, :, ks]
        v = v_ref[0, :, vs]
        l1 = la.astype(BF16)
        r1 = la - l1.astype(F32)
        l2 = r1.astype(BF16)
        l3 = (r1 - l2.astype(F32)).astype(BF16)
        b = (jnp.dot(tril, l1, preferred_element_type=F32) + jnp.dot(tril, l2, preferred_element_type=F32)
             + jnp.dot(tril, l3, preferred_element_type=F32))
        b_last = b[c - 1:c, :]
        st = st_ref[0, h]
        inter = lax.dot_general((q * jnp.exp(b)).astype(BF16), st.astype(BF16), _NT,
                                preferred_element_type=F32)
        scores = jnp.where(level == -1, lax.dot_general(q.astype(BF16), k.astype(BF16), _NT,
                                                        preferred_element_type=F32), 0.0)
        for m in _half_sizes(c):
            ref_rows = _block_ref_rows(b, m)
            qm = (q * jnp.exp(jnp.minimum(b - ref_rows, 0.0))).astype(BF16)
            km = (k * jnp.exp(jnp.minimum(ref_rows - b, 0.0))).astype(BF16)
            scores = jnp.where(level == m.bit_length() - 1,
                               lax.dot_general(qm, km, _NT, preferred_element_type=F32), scores)
        intra = jnp.dot(scores.astype(BF16), v, preferred_element_type=F32)
        o_ref[0, :, vs] = inter + intra
        kd = (k * jnp.exp(b_last - b)).astype(BF16)
        st_ref[0, h] = st * jnp.exp(b_last) + lax.dot_general(v, kd, _TN, preferred_element_type=F32)


def _gla_call(q, k, la, v, s0t, n_valid):
    b, t, _ = q.shape
    c = GLA_CHUNK
    kspec = pl.BlockSpec((1, c, GLA_KW), lambda bi, ci: (bi, ci, 0))
    vspec = pl.BlockSpec((1, c, GLA_VW), lambda bi, ci: (bi, ci, 0))
    sspec = pl.BlockSpec((1, GLA_HEADS, GLA_DV, GLA_DK), lambda bi, ci: (bi, 0, 0, 0))
    return pl.pallas_call(
        functools.partial(_gla_kernel, n_valid=n_valid),
        grid=(b, t // c), in_specs=[kspec, kspec, kspec, vspec, sspec],
        out_specs=[vspec, sspec],
        out_shape=[jax.ShapeDtypeStruct((b, t, GLA_VW), F32),
                   jax.ShapeDtypeStruct((b, GLA_HEADS, GLA_DV, GLA_DK), F32)],
        compiler_params=_cparams(("parallel", "arbitrary")), name="gla")(q, k, la, v, s0t)


def _merge_kernel(h_ref, sb_ref, go_ref, rb_ref, ga_ref, gb_ref, gn_ref, wpa_ref, wpb_ref, wo_ref,
                  g1_ref, b1_ref, out_ref):
    go = go_ref[...]
    parts = []
    for h in range(GLA_HEADS):
        x = go[:, h * GLA_DV:(h + 1) * GLA_DV]
        ms = jnp.mean(x * x, axis=-1, keepdims=True)
        parts.append(x * lax.rsqrt(ms + RMS_EPS) * gn_ref[...])
    o = jnp.concatenate(parts, axis=-1)
    rb = rb_ref[...]
    gated = (o * (rb * _sigmoid(rb))).astype(BF16)
    gla_y = jnp.dot(gated, wpb_ref[...], preferred_element_type=F32)
    sb_y = jnp.dot(sb_ref[...].astype(BF16), wpa_ref[...], preferred_element_type=F32)
    merged = _sigmoid(ga_ref[...]) * sb_y + _sigmoid(gb_ref[...]) * gla_y
    y = jnp.dot(merged.astype(BF16), wo_ref[...], preferred_element_type=F32)
    out_ref[...] = _layer_norm(DN_ALPHA * h_ref[...] + y, g1_ref[...], b1_ref[...])


def _merge_call(h, sb, go, rb, ga, gb, gn, wpa, wpb, wo, g1, b1):
    n, d = h.shape
    tm = min(_row_tile(n), 256)
    row = lambda w: pl.BlockSpec((tm, w), lambda i: (i, 0))
    full = lambda a: pl.BlockSpec(a.shape, lambda i: (0,) * a.ndim)
    args = (h, sb, go, rb, ga, gb, gn, wpa, wpb, wo, g1, b1)
    return pl.pallas_call(
        _merge_kernel, grid=(n // tm,),
        in_specs=[row(d), row(SB_W), row(GLA_VW), row(GLA_VW), row(d), row(d)] + [full(a) for a in args[6:]],
        out_specs=row(d), out_shape=jax.ShapeDtypeStruct((n, d), F32),
        compiler_params=_cparams(("parallel",)), name="merge")(*args)


def _sort_network(n):
    def merge(lo, hi, r):
        step = r * 2
        if step < hi - lo:
            yield from merge(lo, hi, step)
            yield from merge(lo + r, hi, step)
            yield from [(i, i + r) for i in range(lo + r, hi - r, step)]
        else:
            yield (lo, lo + r)

    def sort(lo, hi):
        if hi - lo >= 1:
            mid = lo + (hi - lo) // 2
            yield from sort(lo, mid)
            yield from sort(mid + 1, hi)
            yield from merge(lo, hi, 1)

    return tuple(sort(0, n - 1))


def _bitonic_merge_network(n):
    out, k = [], n // 2
    while k >= 1:
        out.extend((i, i + k) for i in range(n) if not i & k)
        k //= 2
    return tuple(out)


_SORT_TOPK = _sort_network(PEER_TOPK)
_MERGE_TOPK = _bitonic_merge_network(PEER_TOPK)


def _exchange(v, net):
    v = list(v)
    for i, j in net:
        hi, lo = jnp.maximum(v[i], v[j]), jnp.minimum(v[i], v[j])
        v[i], v[j] = hi, lo
    return v


def _merge_top(x, y):
    n = PEER_TOPK
    return _exchange([jnp.maximum(x[k], y[n - 1 - k]) for k in range(n)], _MERGE_TOPK)


def _column_top(s):
    v = _exchange([s[SUBLANES * k:SUBLANES * (k + 1), :] for k in range(PEER_TOPK)], _SORT_TOPK)
    for shift in (4, 2, 1):
        v = _merge_top(v, [pltpu.roll(x, shift, 0) for x in v])
    return v


_CAND_COLS = tuple(PEER_TOPK // (a + 1) for a in range(PEER_TOPK // 2))


def _ranked_top(s):
    rows = s.shape[0]
    iota = lax.broadcasted_iota(jnp.int32, s.shape, 0)
    rank = jnp.full(s.shape, float(PEER_TOPK), F32)
    vals = []
    for k in range(PEER_TOPK):
        m = jnp.max(s, axis=0, keepdims=True)
        first = jnp.min(jnp.where(s == m, iota, rows), axis=0, keepdims=True)
        hit = iota == first
        rank = jnp.where(hit, float(k), rank)
        s = jnp.where(hit, -jnp.inf, s)
        vals.append(m)
    return rank, vals


def _peer_kernel(h_ref, wqt_ref, keys_ref, u_ref, vt_ref, g2_ref, b2_ref, out_ref,
                 qt_scr, s_scr, e1_scr, e2_scr, top_scr, th_scr, stat_scr, flag_scr, acc_scr, *, nc):
    ci = pl.program_id(1)
    tm = h_ref.shape[0]
    nkeys = keys_ref.shape[1]
    hb = h_ref[...].astype(BF16)

    @pl.when(ci == 0)
    def _():
        qt_scr[...] = lax.dot_general(wqt_ref[...], hb, _NT, preferred_element_type=F32)
        top_scr[...] = jnp.zeros_like(top_scr)
        flag_scr[...] = jnp.zeros_like(flag_scr)
        sub = lax.broadcasted_iota(jnp.int32, (SUBLANES, tm), 0)

        def per_half(hs, carry):
            hd = lax.shift_right_logical(hs, 1)
            side = hs & 1
            qh = qt_scr[pl.ds(pl.multiple_of(hs * PEER_HALF, PEER_HALF), PEER_HALF), :]
            s = jnp.dot(keys_ref[side], qh, preferred_element_type=F32)
            s_scr[hs] = s
            top = _column_top(s)
            at_least = jnp.sum(jnp.where(s >= top[PEER_TOPK - 1][0:1, :], 1.0, 0.0), axis=0, keepdims=True)
            tie = jnp.where(at_least != float(PEER_TOPK), 1.0, flag_scr[...])
            for k in range(PEER_TOPK):
                top_scr[side, k] = jnp.where(sub == hd, top[k], top_scr[side, k])
                if k:
                    tie = jnp.where(top[k - 1] == top[k], 1.0, tie)
            flag_scr[...] = tie
            return carry

        lax.fori_loop(0, 2 * PEER_HEADS, per_half, 0)

        t1 = [top_scr[0, k] for k in range(PEER_TOPK)]
        t2 = [top_scr[1, k] for k in range(PEER_TOPK)]
        neg = jnp.full((SUBLANES, tm), -jnp.inf, F32)
        lists = [[t1[a] + t2[b] for b in range(nb)] + [neg] * (PEER_TOPK - nb) for a, nb in enumerate(_CAND_COLS)]
        half = PEER_TOPK // 2
        lists.append([t1[a] + t2[0] for a in range(half, PEER_TOPK)] + [neg] * half)
        while len(lists) > 1:
            lists = [_merge_top(lists[i], lists[i + 1]) if i + 1 < len(lists) else lists[i]
                     for i in range(0, len(lists), 2)]
        cv = lists[0]
        tau = cv[PEER_TOPK - 1]
        zsum = jnp.exp(cv[0] - cv[0])
        for k in range(1, PEER_TOPK):
            zsum = zsum + jnp.exp(cv[k] - cv[0])
        stat_scr[1] = zsum
        stat_scr[2] = t1[0]
        stat_scr[3] = t2[0]
        n_reach = jnp.zeros((SUBLANES, tm), F32)
        for a in range(PEER_TOPK):
            th = jnp.full((SUBLANES, tm), jnp.inf, F32)
            for b in range(PEER_TOPK // (a + 1)):
                reach = t1[a] + t2[b] >= tau
                th = jnp.where(reach, t2[b], th)
                n_reach = n_reach + jnp.where(reach, 1.0, 0.0)
            th_scr[a] = th
        any_tie = jnp.max(jnp.where(n_reach != float(PEER_TOPK), 1.0, flag_scr[...])) > 0.0

        @pl.when(jnp.logical_not(any_tie))
        def _():
            def per_head(hd, carry):
                zr = stat_scr[1, pl.ds(hd, 1), :]
                m1 = stat_scr[2, pl.ds(hd, 1), :]
                m2 = stat_scr[3, pl.ds(hd, 1), :]
                s1 = s_scr[2 * hd]
                e1_scr[hd] = jnp.exp(s1 - m1) / zr
                e2_scr[hd] = jnp.exp(s_scr[2 * hd + 1] - m2)
                thr = jnp.full((nkeys, tm), jnp.inf, F32)
                for a in range(PEER_TOPK):
                    thr = jnp.where(s1 == top_scr[0, a, pl.ds(hd, 1), :], th_scr[a, pl.ds(hd, 1), :], thr)
                s_scr[2 * hd] = thr
                return carry

            lax.fori_loop(0, PEER_HEADS, per_head, 0)

        @pl.when(any_tie)
        def _():
            def per_head(hd, carry):
                s1 = s_scr[2 * hd]
                s2 = s_scr[2 * hd + 1]
                rank1, v1 = _ranked_top(s1)
                rank2, v2 = _ranked_top(s2)
                v2col = jnp.concatenate(v2, axis=0)
                cand = jnp.concatenate([v1[a] + v2col for a in range(PEER_TOPK)], axis=0)
                crank, cvals = _ranked_top(cand)
                taken = jnp.where(crank < float(PEER_TOPK), 1.0, 0.0)
                pairs = jnp.zeros((nkeys, tm), F32)
                for a in range(PEER_TOPK):
                    n_a = jnp.sum(taken[a * PEER_TOPK:(a + 1) * PEER_TOPK, :], axis=0, keepdims=True)
                    pairs = jnp.where(rank1 == float(a), n_a, pairs)
                zs = jnp.exp(cvals[0] - cvals[0])
                for k in range(1, PEER_TOPK):
                    zs = zs + jnp.exp(cvals[k] - cvals[0])
                e1_scr[hd] = jnp.exp(s1 - v1[0]) / zs
                e2_scr[hd] = jnp.exp(s2 - v2[0])
                s_scr[2 * hd] = jnp.where(pairs > 0.0, 1.0 - pairs, jnp.inf)
                s_scr[2 * hd + 1] = jnp.where(rank2 < float(PEER_TOPK), -rank2, -jnp.inf)
                return carry

            lax.fori_loop(0, PEER_HEADS, per_head, 0)

        acc_scr[...] = jnp.zeros_like(acc_scr)

    piece = 2 * nkeys
    npiece = PEER_GROUP // 2

    def pre_piece(pi):
        return lax.dot_general(u_ref[pi * piece:(pi + 1) * piece, :], hb, _NT, preferred_element_type=F32)

    def gate_rows(i1):
        w = jnp.zeros((nkeys, tm), F32)
        for hd in range(PEER_HEADS):
            thr = s_scr[2 * hd, pl.ds(i1, 1), :]
            e1r = e1_scr[hd, pl.ds(i1, 1), :]
            w = w + jnp.where(s_scr[2 * hd + 1] >= thr, e2_scr[hd] * e1r, 0.0)
        return w

    for pi in range(npiece):
        pre = pre_piece(pi)
        i1 = ci * PEER_GROUP + 2 * pi
        w = jnp.concatenate([gate_rows(i1), gate_rows(i1 + 1)], axis=0)
        act = 0.5 * pre * (1.0 + lax.erf(pre * (2.0 ** -0.5)))
        a = (act * w).astype(BF16)
        acc_scr[...] += jnp.dot(vt_ref[:, pi * piece:(pi + 1) * piece], a, preferred_element_type=F32)

    @pl.when(ci == nc - 1)
    def _():
        y = acc_scr[...].T
        out_ref[...] = _layer_norm(DN_ALPHA * h_ref[...] + y, g2_ref[...], b2_ref[...])


def _peer_call(h, wqt, keys, u, vt, g2, b2):
    n, d = h.shape
    nkeys = keys.shape[1]
    tm = min(_row_tile(n), 256)
    g = PEER_GROUP
    nc = nkeys // g
    hspec = pl.BlockSpec((tm, d), lambda i, c: (i, 0))
    full = lambda a: pl.BlockSpec(a.shape, lambda i, c: (0,) * a.ndim)
    assert nkeys == PEER_TOPK * SUBLANES and PEER_HEADS == SUBLANES
    dense = pltpu.VMEM((PEER_HEADS, nkeys, tm), F32)
    return pl.pallas_call(
        functools.partial(_peer_kernel, nc=nc), grid=(n // tm, nc),
        in_specs=[hspec, full(wqt), full(keys),
                  pl.BlockSpec((g * nkeys, d), lambda i, c: (c, 0)),
                  pl.BlockSpec((d, g * nkeys), lambda i, c: (0, c)), full(g2), full(b2)],
        out_specs=hspec, out_shape=jax.ShapeDtypeStruct((n, d), F32),
        scratch_shapes=[pltpu.VMEM((PEER_HEADS * 2 * PEER_HALF, tm), F32),
                        pltpu.VMEM((2 * PEER_HEADS, nkeys, tm), F32), dense, dense,
                        pltpu.VMEM((2, PEER_TOPK, SUBLANES, tm), F32), pltpu.VMEM((PEER_TOPK, SUBLANES, tm), F32),
                        pltpu.VMEM((4, SUBLANES, tm), F32), pltpu.VMEM((SUBLANES, tm), F32),
                        pltpu.VMEM((d, tm), F32)],
        compiler_params=_cparams(("parallel", "arbitrary")), name="peer")(h, wqt, keys, u, vt, g2, b2)


def _round_up(x, m):
    return -(-x // m) * m


def _pad_rows(x, rows):
    return jnp.pad(x, ((0, 0), (0, rows - x.shape[1]), (0, 0)))


def _pad_flat(x, rows):
    return jnp.pad(x, ((0, rows - x.shape[0]), (0, 0)))


def kernel(x_prompt, x_sample, cache_k, cache_v, state_gla, meta, ln_in_g, ln_in_b, w_in, w_gk2, b_gk,
           gla_norm_g, w_pa, w_pb, w_o, ln1_g, ln1_b, peer_wq, peer_subkeys, peer_u, peer_v, ln2_g, ln2_b):
    bp, s_len, d = x_prompt.shape
    bd, t_new, _ = x_sample.shape
    past = cache_k.shape[2]
    n_len = N_META + s_len
    keep = min(s_len, past)
    lp = _round_up(n_len, ATTN_BLOCK)
    ts = _round_up(t_new, ATTN_BLOCK)
    tg = _round_up(t_new, GLA_CHUNK)
    tk_s = _round_up(N_META + past + t_new, ATTN_BLOCK)
    row = lambda a: a.reshape(1, -1)

    meta_rows = jnp.broadcast_to(meta[None].astype(x_prompt.dtype), (bp, N_META, d))
    xp = _pad_rows(jnp.concatenate([meta_rows, x_prompt], axis=1), lp).reshape(bp * lp, d)
    ns = bd * t_new
    ns_p = _round_up(ns, LANES)
    xs = _pad_flat(x_sample.reshape(ns, d), ns_p)
    hp = _ln_call(xp, ln_in_g, ln_in_b)
    hs = _ln_call(xs, ln_in_g, ln_in_b)

    outs = {k: [] for k in ("nk_p", "nv_p", "st_p", "nk_s", "nv_s", "st_s")}
    for l in range(DEPTH):
        glr0 = _SEC[7]
        w_main = jnp.concatenate([w_in[l][:, :glr0], w_in[l][:, glr0 + GLA_RANK:]], axis=1).astype(BF16)
        w_glr = jnp.pad(w_in[l][:, glr0:glr0 + GLA_RANK], ((0, 0), (0, LANES - GLA_RANK))).astype(BF16)
        w_gk2p = jnp.pad(w_gk2[l], ((0, LANES - GLA_RANK), (0, 0)))
        merge_w = (row(gla_norm_g[l]), w_pa[l].astype(BF16), w_pb[l].astype(BF16), w_o[l].astype(BF16),
                   row(ln1_g[l]), row(ln1_b[l]))
        peer_w = (peer_wq[l].T.astype(BF16), peer_subkeys[l], peer_u[l].astype(BF16),
                  peer_v[l].T.astype(BF16), row(ln2_g[l]), row(ln2_b[l]))

        qa, kab, vab, ka, va, qb, kb, vb, la, rb, ga, gb = _inproj_call(hp, w_main, w_glr, w_gk2p, row(b_gk[l]))
        b3 = lambda a: a.reshape(bp, lp, a.shape[-1])
        sb = _sb_attn_call(b3(qa), b3(kab), b3(vab), 0)
        s0t = jnp.zeros((bp, GLA_HEADS, GLA_DV, GLA_DK), F32)
        go, stp = _gla_call(b3(qb), b3(kb), b3(la), b3(vb), s0t, n_len)
        hp = _merge_call(hp, sb.reshape(bp * lp, SB_W), go.reshape(bp * lp, GLA_VW), rb, ga, gb, *merge_w)
        hp = _peer_call(hp, *peer_w)
        ka3 = ka.reshape(bp, lp, SB_HEADS, SB_DIM)
        va3 = va.reshape(bp, lp, SB_HEADS, SB_DIM)
        outs["nk_p"].append(ka3[:, n_len - keep:n_len])
        outs["nv_p"].append(va3[:, n_len - keep:n_len])
        outs["st_p"].append(jnp.swapaxes(stp, 2, 3).astype(state_gla.dtype))

        qa2, _, _, ka2, va2, qb2, kb2, vb2, la2, rb2, ga2, gb2 = _inproj_call(
            hs, w_main, w_glr, w_gk2p, row(b_gk[l]))
        s3 = lambda a: a[:ns].reshape(bd, t_new, a.shape[-1])
        flat = lambda a: _pad_flat(a[:, :t_new].reshape(ns, a.shape[-1]), ns_p)
        ka2n = s3(ka2)
        va2n = s3(va2)
        meta_k = jnp.broadcast_to(ka3[:1, :N_META].reshape(1, N_META, SB_W), (bd, N_META, SB_W))
        meta_v = jnp.broadcast_to(va3[:1, :N_META].reshape(1, N_META, SB_W), (bd, N_META, SB_W))
        k_all = jnp.concatenate([meta_k, cache_k[l].reshape(bd, past, SB_W), ka2n], axis=1)
        v_all = jnp.concatenate([meta_v, cache_v[l].reshape(bd, past, SB_W), va2n], axis=1)
        sb2 = _sb_attn_call(_pad_rows(s3(qa2), ts), _pad_rows(k_all, tk_s).astype(BF16),
                            _pad_rows(v_all, tk_s).astype(BF16), N_META + past)
        g3 = lambda a: _pad_rows(s3(a), tg)
        go2, sts = _gla_call(g3(qb2), g3(kb2), g3(la2), g3(vb2), jnp.swapaxes(state_gla[l], 2, 3), t_new)
        hs = _merge_call(hs, flat(sb2), flat(go2), rb2, ga2, gb2, *merge_w)
        hs = _peer_call(hs, *peer_w)
        outs["nk_s"].append(ka2n.reshape(bd, t_new, SB_HEADS, SB_DIM))
        outs["nv_s"].append(va2n.reshape(bd, t_new, SB_HEADS, SB_DIM))
        outs["st_s"].append(jnp.swapaxes(sts, 2, 3).astype(state_gla.dtype))

    y_prompt = hp.reshape(bp, lp, d)[:, N_META:n_len]
    y_sample = hs[:ns].reshape(bd, t_new, d)
    return (y_prompt, y_sample, jnp.stack(outs["nk_p"]), jnp.stack(outs["nv_p"]), jnp.stack(outs["st_p"]),
            jnp.stack(outs["nk_s"]), jnp.stack(outs["nv_s"]), jnp.stack(outs["st_s"]))
```

```python
import functools

import jax
import jax.numpy as jnp
from jax import lax
from jax.experimental import pallas as pl
from jax.experimental.pallas import tpu as pltpu

F32 = jnp.float32
BF16 = jnp.bfloat16

D_MODEL = 1024
DEPTH = 2
N_META = 16
SB_HEADS = 8
SB_DIM = 64
SB_W = SB_HEADS * SB_DIM
GLA_HEADS = 4
GLA_DK = 128
GLA_DV = 256
GLA_KW = GLA_HEADS * GLA_DK
GLA_VW = GLA_HEADS * GLA_DV
GLA_RANK = 16
GLA_TAU = 16.0
PEER_HEADS = 8
PEER_HALF = 128
PEER_TOPK = 16
DN_ALPHA = float((2 * DEPTH) ** 0.25)
LN_EPS = 1e-5
RMS_EPS = 1e-6

LANES = 128
SUBLANES = 8
VMEM_LIMIT_BYTES = 56 * 1024 * 1024

ATTN_BLOCK = 128
GLA_CHUNK = 128
PEER_GROUP = 32
PEER_STREAMS = 4
EXP_ZERO_BELOW = -104.0

_NT = (((1,), (1,)), ((), ()))
_TN = (((0,), (0,)), ((), ()))


def _cparams(sem, flags=None):
    return pltpu.CompilerParams(dimension_semantics=sem, vmem_limit_bytes=VMEM_LIMIT_BYTES, flags=flags)


def _row_tile(n):
    for t in (512, 256, 128):
        if n % t == 0:
            return t
    raise ValueError(f"token count {n} is not a multiple of 128")


def _layer_norm(x, g, b):
    mu = jnp.mean(x, axis=-1, keepdims=True)
    xc = x - mu
    var = jnp.mean(xc * xc, axis=-1, keepdims=True)
    return xc * lax.rsqrt(var + LN_EPS) * g + b


def _softplus(x):
    return jnp.maximum(x, 0.0) + jnp.log(1.0 + jnp.exp(-jnp.abs(x)))


def _sigmoid(x):
    return 1.0 / (1.0 + jnp.exp(-x))


def _ln_kernel(x_ref, g_ref, b_ref, o_ref):
    o_ref[...] = _layer_norm(x_ref[...], g_ref[...], b_ref[...])


def _ln_call(x, g, b):
    n, d = x.shape
    tm = _row_tile(n)
    row = pl.BlockSpec((tm, d), lambda i: (i, 0))
    vec = pl.BlockSpec((1, d), lambda i: (0, 0))
    return pl.pallas_call(
        _ln_kernel, grid=(n // tm,), in_specs=[row, vec, vec], out_specs=row,
        out_shape=jax.ShapeDtypeStruct((n, d), F32), compiler_params=_cparams(("parallel",)),
        name="ln_in")(x, g.reshape(1, d), b.reshape(1, d))


_SEC = (0, 512, 1024, 1536, 2048, 2560, 3584, 4608, 5632, 6656)


def _inproj_kernel(h_ref, w_ref, wg_ref, wgk2_ref, bgk_ref,
                   qa_ref, kab_ref, vab_ref, ka_ref, va_ref, qb_ref, kb_ref, vb_ref, la_ref,
                   rb_ref, ga_ref, gb_ref):
    hb = h_ref[...].astype(BF16)

    def sec(i):
        return jnp.dot(hb, w_ref[:, _SEC[i]:_SEC[i + 1]], preferred_element_type=F32)

    qa_ref[...] = (sec(0) * (SB_DIM ** -0.5)).astype(BF16)
    ka = sec(1)
    ka_ref[...] = ka
    kab_ref[...] = ka.astype(BF16)
    va = sec(2)
    va_ref[...] = va
    vab_ref[...] = va.astype(BF16)
    qb_ref[...] = sec(3) * (GLA_DK ** -0.5)
    kb_ref[...] = sec(4)
    vb_ref[...] = sec(5).astype(BF16)
    rb_ref[...] = sec(6)
    ga_ref[...] = sec(7)
    gb_ref[...] = sec(8)
    glr = jnp.dot(hb, wg_ref[...], preferred_element_type=F32)
    x = jnp.dot(glr, wgk2_ref[...], preferred_element_type=F32) + bgk_ref[...]
    la_ref[...] = -_softplus(-x) * (1.0 / GLA_TAU)


def _inproj_call(h, w_main, w_glr, w_gk2p, b_gk):
    n, d = h.shape
    tm = min(_row_tile(n), 256)
    row = lambda w: pl.BlockSpec((tm, w), lambda i: (i, 0))
    full = lambda a: pl.BlockSpec(a.shape, lambda i: (0,) * a.ndim)
    outs = [(SB_W, BF16), (SB_W, BF16), (SB_W, BF16), (SB_W, F32), (SB_W, F32),
            (GLA_KW, F32), (GLA_KW, F32), (GLA_VW, BF16), (GLA_KW, F32),
            (GLA_VW, F32), (D_MODEL, F32), (D_MODEL, F32)]
    return pl.pallas_call(
        _inproj_kernel, grid=(n // tm,),
        in_specs=[row(d), full(w_main), full(w_glr), full(w_gk2p), full(b_gk)],
        out_specs=[row(w) for w, _ in outs],
        out_shape=[jax.ShapeDtypeStruct((n, w), t) for w, t in outs],
        compiler_params=_cparams(("parallel",)), name="in_proj")(h, w_main, w_glr, w_gk2p, b_gk)


def _sb_attn_kernel(q_ref, k_ref, v_ref, o_ref, carry_scr, acc_scr, *, q_off, nk):
    tq = tk = ATTN_BLOCK
    npair = SB_W // LANES
    i = pl.program_id(1)
    lane = lax.broadcasted_iota(jnp.int32, (tq, LANES), 1)
    r_iota = lax.broadcasted_iota(jnp.int32, (tq, tk), 0)
    c_iota = lax.broadcasted_iota(jnp.int32, (tq, tk), 1)
    later_ones = jnp.concatenate([(r_iota > c_iota).astype(BF16), jnp.ones((tk, tk), BF16)], axis=1)
    qpos = q_off + i * tq + r_iota
    j_start = jnp.minimum((q_off + i * tq + tq - 2) // tk, nk - 1)
    carry_scr[...] = jnp.zeros_like(carry_scr)
    acc_scr[...] = jnp.zeros_like(acc_scr)

    def cond(c):
        j, cmax = c
        return jnp.logical_and(j >= 0, cmax > EXP_ZERO_BELOW)

    def body(c):
        j, _ = c
        off = pl.multiple_of(j * tk, tk)
        mask = (off + c_iota) < qpos
        heads = range(SB_HEADS)
        zs = []
        for h in heads:
            cols = slice((h // 2) * LANES, (h // 2 + 1) * LANES)
            q = q_ref[0, :, cols]
            qm = jnp.where(lane < SB_DIM if h % 2 == 0 else lane >= SB_DIM, q, jnp.zeros_like(q))
            zs.append(lax.dot_general(qm, k_ref[0, pl.ds(off, tk), cols], _NT, preferred_element_type=F32))
        lbs, css = [], []
        for h in heads:
            sp = _softplus(zs[h])
            lbs.append(zs[h] - sp)
            lk = jnp.where(mask, -sp, 0.0)
            hi = lk.astype(BF16)
            lo = (lk - hi.astype(F32)).astype(BF16)
            css.append(jnp.dot(hi, later_ones, preferred_element_type=F32)
                       + jnp.dot(lo, later_ones, preferred_element_type=F32))
        cmax = None
        pvs = []
        for h in heads:
            cols = slice((h // 2) * LANES, (h // 2 + 1) * LANES)
            carry = carry_scr[h]
            w = jnp.where(mask, jnp.exp(lbs[h] + css[h][:, :tk] + carry), 0.0).astype(BF16)
            pvs.append(jnp.dot(w, v_ref[0, pl.ds(off, tk), cols], preferred_element_type=F32))
            carry = carry + css[h][:, tk:]
            carry_scr[h] = carry
            cmax = carry if cmax is None else jnp.maximum(cmax, carry)
        for p in range(npair):
            acc_scr[p] += jnp.where(lane < SB_DIM, pvs[2 * p], pvs[2 * p + 1])
        return j - 1, jnp.max(cmax)

    lax.while_loop(cond, body, (j_start, jnp.float32(0.0)))
    for p in range(npair):
        o_ref[0, :, p * LANES:(p + 1) * LANES] = acc_scr[p]


def _sb_attn_call(q, k, v, q_off):
    b, tq_all, _ = q.shape
    tk_all = k.shape[1]
    nq, nk = tq_all // ATTN_BLOCK, tk_all // ATTN_BLOCK
    qspec = pl.BlockSpec((1, ATTN_BLOCK, SB_W), lambda bi, i: (bi, i, 0))
    kspec = pl.BlockSpec((1, tk_all, SB_W), lambda bi, i: (bi, 0, 0))
    return pl.pallas_call(
        functools.partial(_sb_attn_kernel, q_off=q_off, nk=nk),
        grid=(b, nq), in_specs=[qspec, kspec, kspec], out_specs=qspec,
        out_shape=jax.ShapeDtypeStruct((b, tq_all, SB_W), F32),
        scratch_shapes=[pltpu.VMEM((SB_HEADS, ATTN_BLOCK, ATTN_BLOCK), F32),
                        pltpu.VMEM((SB_W // LANES, ATTN_BLOCK, LANES), F32)],
        compiler_params=_cparams(("parallel", "arbitrary")), name="sb_attn")(q, k, v)


def _half_sizes(c):
    out, m = [], c // 2
    while m >= 1:
        out.append(m)
        m //= 2
    return out


def _block_ref_rows(b, m):
    c, dk = b.shape
    if 2 * m >= SUBLANES:
        pieces = [jnp.broadcast_to(b[a + m - 1:a + m, :], (2 * m, dk)) for a in range(0, c, 2 * m)]
        return pieces[0] if len(pieces) == 1 else jnp.concatenate(pieces, axis=0)
    row8 = lax.broadcasted_iota(jnp.int32, (SUBLANES, dk), 0)
    tiles = []
    for a in range(0, c, SUBLANES):
        tile = None
        for off in range(0, SUBLANES, 2 * m):
            r = jnp.broadcast_to(b[a + off + m - 1:a + off + m, :], (SUBLANES, dk))
            tile = r if tile is None else jnp.where(row8 >= off, r, tile)
        tiles.append(tile)
    return jnp.concatenate(tiles, axis=0)


def _gla_kernel(q_ref, k_ref, la_ref, v_ref, s0_ref, o_ref, st_ref, *, n_valid):
    c = GLA_CHUNK
    ci = pl.program_id(1)

    @pl.when(ci == 0)
    def _():
        st_ref[...] = s0_ref[...]

    r_iota = lax.broadcasted_iota(jnp.int32, (c, c), 0)
    c_iota = lax.broadcasted_iota(jnp.int32, (c, c), 1)
    tril = (r_iota >= c_iota).astype(BF16)
    level = jnp.where(c_iota < r_iota, 31 - lax.clz(r_iota ^ c_iota), jnp.where(c_iota == r_iota, -1, -2))
    valid = (ci * c + lax.broadcasted_iota(jnp.int32, (c, GLA_DK), 0)) < n_valid

    for h in range(GLA_HEADS):
        ks = slice(h * GLA_DK, (h + 1) * GLA_DK)
        vs = slice(h * GLA_DV, (h + 1) * GLA_DV)
        la = jnp.where(valid, la_ref[0, :, ks], 0.0)
        k = jnp.where(valid, k_ref[0, :, ks], 0.0)
        q = q_ref[0, :, ks]
        v = v_ref[0, :, vs]
        l1 = la.astype(BF16)
        r1 = la - l1.astype(F32)
        l2 = r1.astype(BF16)
        l3 = (r1 - l2.astype(F32)).astype(BF16)
        b = (jnp.dot(tril, l1, preferred_element_type=F32) + jnp.dot(tril, l2, preferred_element_type=F32)
             + jnp.dot(tril, l3, preferred_element_type=F32))
        b_last = b[c - 1:c, :]
        st = st_ref[0, h]
        inter = lax.dot_general((q * jnp.exp(b)).astype(BF16), st.astype(BF16), _NT,
                                preferred_element_type=F32)
        scores = jnp.where(level == -1, lax.dot_general(q.astype(BF16), k.astype(BF16), _NT,
                                                        preferred_element_type=F32), 0.0)
        for m in _half_sizes(c):
            ref_rows = _block_ref_rows(b, m)
            qm = (q * jnp.exp(jnp.minimum(b - ref_rows, 0.0))).astype(BF16)
            km = (k * jnp.exp(jnp.minimum(ref_rows - b, 0.0))).astype(BF16)
            scores = jnp.where(level == m.bit_length() - 1,
                               lax.dot_general(qm, km, _NT, preferred_element_type=F32), scores)
        intra = jnp.dot(scores.astype(BF16), v, preferred_element_type=F32)
        o_ref[0, :, vs] = inter + intra
        kd = (k * jnp.exp(b_last - b)).astype(BF16)
        st_ref[0, h] = st * jnp.exp(b_last) + lax.dot_general(v, kd, _TN, preferred_element_type=F32)


def _gla_call(q, k, la, v, s0t, n_valid):
    b, t, _ = q.shape
    c = GLA_CHUNK
    kspec = pl.BlockSpec((1, c, GLA_KW), lambda bi, ci: (bi, ci, 0))
    vspec = pl.BlockSpec((1, c, GLA_VW), lambda bi, ci: (bi, ci, 0))
    sspec = pl.BlockSpec((1, GLA_HEADS, GLA_DV, GLA_DK), lambda bi, ci: (bi, 0, 0, 0))
    return pl.pallas_call(
        functools.partial(_gla_kernel, n_valid=n_valid),
        grid=(b, t // c), in_specs=[kspec, kspec, kspec, vspec, sspec],
        out_specs=[vspec, sspec],
        out_shape=[jax.ShapeDtypeStruct((b, t, GLA_VW), F32),
                   jax.ShapeDtypeStruct((b, GLA_HEADS, GLA_DV, GLA_DK), F32)],
        compiler_params=_cparams(("parallel", "arbitrary")), name="gla")(q, k, la, v, s0t)


def _merge_kernel(h_ref, sb_ref, go_ref, rb_ref, ga_ref, gb_ref, gn_ref, wpa_ref, wpb_ref, wo_ref,
                  g1_ref, b1_ref, out_ref):
    go = go_ref[...]
    parts = []
    for h in range(GLA_HEADS):
        x = go[:, h * GLA_DV:(h + 1) * GLA_DV]
        ms = jnp.mean(x * x, axis=-1, keepdims=True)
        parts.append(x * lax.rsqrt(ms + RMS_EPS) * gn_ref[...])
    o = jnp.concatenate(parts, axis=-1)
    rb = rb_ref[...]
    gated = (o * (rb * _sigmoid(rb))).astype(BF16)
    gla_y = jnp.dot(gated, wpb_ref[...], preferred_element_type=F32)
    sb_y = jnp.dot(sb_ref[...].astype(BF16), wpa_ref[...], preferred_element_type=F32)
    merged = _sigmoid(ga_ref[...]) * sb_y + _sigmoid(gb_ref[...]) * gla_y
    y = jnp.dot(merged.astype(BF16), wo_ref[...], preferred_element_type=F32)
    out_ref[...] = _layer_norm(DN_ALPHA * h_ref[...] + y, g1_ref[...], b1_ref[...])


def _merge_call(h, sb, go, rb, ga, gb, gn, wpa, wpb, wo, g1, b1):
    n, d = h.shape
    tm = min(_row_tile(n), 256)
    row = lambda w: pl.BlockSpec((tm, w), lambda i: (i, 0))
    full = lambda a: pl.BlockSpec(a.shape, lambda i: (0,) * a.ndim)
    args = (h, sb, go, rb, ga, gb, gn, wpa, wpb, wo, g1, b1)
    return pl.pallas_call(
        _merge_kernel, grid=(n // tm,),
        in_specs=[row(d), row(SB_W), row(GLA_VW), row(GLA_VW), row(d), row(d)] + [full(a) for a in args[6:]],
        out_specs=row(d), out_shape=jax.ShapeDtypeStruct((n, d), F32),
        compiler_params=_cparams(("parallel",)), name="merge")(*args)


def _sort_network(n):
    def merge(lo, hi, r):
        step = r * 2
        if step < hi - lo:
            yield from merge(lo, hi, step)
            yield from merge(lo + r, hi, step)
            yield from [(i, i + r) for i in range(lo + r, hi - r, step)]
        else:
            yield (lo, lo + r)

    def sort(lo, hi):
        if hi - lo >= 1:
            mid = lo + (hi - lo) // 2
            yield from sort(lo, mid)
            yield from sort(mid + 1, hi)
            yield from merge(lo, hi, 1)

    return tuple(sort(0, n - 1))


def _bitonic_merge_network(n):
    out, k = [], n // 2
    while k >= 1:
        out.extend((i, i + k) for i in range(n) if not i & k)
        k //= 2
    return tuple(out)


_SORT_TOPK = _sort_network(PEER_TOPK)
_MERGE_TOPK = _bitonic_merge_network(PEER_TOPK)


def _exchange(v, net):
    v = list(v)
    for i, j in net:
        hi, lo = jnp.maximum(v[i], v[j]), jnp.minimum(v[i], v[j])
        v[i], v[j] = hi, lo
    return v


def _merge_top(x, y):
    n = PEER_TOPK
    return _exchange([jnp.maximum(x[k], y[n - 1 - k]) for k in range(n)], _MERGE_TOPK)


def _column_top(s):
    v = _exchange([s[SUBLANES * k:SUBLANES * (k + 1), :] for k in range(PEER_TOPK)], _SORT_TOPK)
    for shift in (4, 2, 1):
        v = _merge_top(v, [pltpu.roll(x, shift, 0) for x in v])
    return v


_CAND_COLS = tuple(PEER_TOPK // (a + 1) for a in range(PEER_TOPK // 2))


def _ranked_top(s):
    rows = s.shape[0]
    iota = lax.broadcasted_iota(jnp.int32, s.shape, 0)
    rank = jnp.full(s.shape, float(PEER_TOPK), F32)
    vals = []
    for k in range(PEER_TOPK):
        m = jnp.max(s, axis=0, keepdims=True)
        first = jnp.min(jnp.where(s == m, iota, rows), axis=0, keepdims=True)
        hit = iota == first
        rank = jnp.where(hit, float(k), rank)
        s = jnp.where(hit, -jnp.inf, s)
        vals.append(m)
    return rank, vals


def _peer_kernel(h_ref, wqt_ref, keys_ref, g2_ref, b2_ref, *rest, nc):
    ns = PEER_STREAMS
    u_refs, vt_refs = rest[:ns], rest[ns:2 * ns]
    out_ref, qt_scr, s_scr, e1_scr, e2_scr, top_scr, th_scr, stat_scr, flag_scr, acc_scr = rest[2 * ns:]
    _peer_body(h_ref, wqt_ref, keys_ref, u_refs, vt_refs, g2_ref, b2_ref, out_ref,
               qt_scr, s_scr, e1_scr, e2_scr, top_scr, th_scr, stat_scr, flag_scr, acc_scr, nc=nc)


def _peer_body(h_ref, wqt_ref, keys_ref, u_refs, vt_refs, g2_ref, b2_ref, out_ref,
               qt_scr, s_scr, e1_scr, e2_scr, top_scr, th_scr, stat_scr, flag_scr, acc_scr, *, nc):
    ci = pl.program_id(1)
    tm = h_ref.shape[0]
    nkeys = keys_ref.shape[1]
    hb = h_ref[...].astype(BF16)

    @pl.when(ci == 0)
    def _():
        qt_scr[...] = lax.dot_general(wqt_ref[...], hb, _NT, preferred_element_type=F32)
        top_scr[...] = jnp.zeros_like(top_scr)
        flag_scr[...] = jnp.zeros_like(flag_scr)
        sub = lax.broadcasted_iota(jnp.int32, (SUBLANES, tm), 0)

        def per_half(hs, carry):
            hd = lax.shift_right_logical(hs, 1)
            side = hs & 1
            qh = qt_scr[pl.ds(pl.multiple_of(hs * PEER_HALF, PEER_HALF), PEER_HALF), :]
            s = jnp.dot(keys_ref[side], qh, preferred_element_type=F32)
            s_scr[hs] = s
            top = _column_top(s)
            at_least = jnp.sum(jnp.where(s >= top[PEER_TOPK - 1][0:1, :], 1.0, 0.0), axis=0, keepdims=True)
            tie = jnp.where(at_least != float(PEER_TOPK), 1.0, flag_scr[...])
            for k in range(PEER_TOPK):
                top_scr[side, k] = jnp.where(sub == hd, top[k], top_scr[side, k])
                if k:
                    tie = jnp.where(top[k - 1] == top[k], 1.0, tie)
            flag_scr[...] = tie
            return carry

        lax.fori_loop(0, 2 * PEER_HEADS, per_half, 0)

        t1 = [top_scr[0, k] for k in range(PEER_TOPK)]
        t2 = [top_scr[1, k] for k in range(PEER_TOPK)]
        neg = jnp.full((SUBLANES, tm), -jnp.inf, F32)
        lists = [[t1[a] + t2[b] for b in range(nb)] + [neg] * (PEER_TOPK - nb) for a, nb in enumerate(_CAND_COLS)]
        half = PEER_TOPK // 2
        lists.append([t1[a] + t2[0] for a in range(half, PEER_TOPK)] + [neg] * half)
        while len(lists) > 1:
            lists = [_merge_top(lists[i], lists[i + 1]) if i + 1 < len(lists) else lists[i]
                     for i in range(0, len(lists), 2)]
        cv = lists[0]
        tau = cv[PEER_TOPK - 1]
        zsum = jnp.exp(cv[0] - cv[0])
        for k in range(1, PEER_TOPK):
            zsum = zsum + jnp.exp(cv[k] - cv[0])
        stat_scr[1] = zsum
        stat_scr[2] = t1[0]
        stat_scr[3] = t2[0]
        n_reach = jnp.zeros((SUBLANES, tm), F32)
        for a in range(PEER_TOPK):
            th = jnp.full((SUBLANES, tm), jnp.inf, F32)
            for b in range(PEER_TOPK // (a + 1)):
                reach = t1[a] + t2[b] >= tau
                th = jnp.where(reach, t2[b], th)
                n_reach = n_reach + jnp.where(reach, 1.0, 0.0)
            th_scr[a] = th
        any_tie = jnp.max(jnp.where(n_reach != float(PEER_TOPK), 1.0, flag_scr[...])) > 0.0

        @pl.when(jnp.logical_not(any_tie))
        def _():
            def per_head(hd, carry):
                zr = stat_scr[1, pl.ds(hd, 1), :]
                m1 = stat_scr[2, pl.ds(hd, 1), :]
                m2 = stat_scr[3, pl.ds(hd, 1), :]
                s1 = s_scr[2 * hd]
                e1_scr[hd] = jnp.exp(s1 - m1) / zr
                e2_scr[hd] = jnp.exp(s_scr[2 * hd + 1] - m2)
                thr = jnp.full((nkeys, tm), jnp.inf, F32)
                for a in range(PEER_TOPK):
                    thr = jnp.where(s1 == top_scr[0, a, pl.ds(hd, 1), :], th_scr[a, pl.ds(hd, 1), :], thr)
                s_scr[2 * hd] = thr
                return carry

            lax.fori_loop(0, PEER_HEADS, per_head, 0)

        @pl.when(any_tie)
        def _():
            def per_head(hd, carry):
                s1 = s_scr[2 * hd]
                s2 = s_scr[2 * hd + 1]
                rank1, v1 = _ranked_top(s1)
                rank2, v2 = _ranked_top(s2)
                v2col = jnp.concatenate(v2, axis=0)
                cand = jnp.concatenate([v1[a] + v2col for a in range(PEER_TOPK)], axis=0)
                crank, cvals = _ranked_top(cand)
                taken = jnp.where(crank < float(PEER_TOPK), 1.0, 0.0)
                pairs = jnp.zeros((nkeys, tm), F32)
                for a in range(PEER_TOPK):
                    n_a = jnp.sum(taken[a * PEER_TOPK:(a + 1) * PEER_TOPK, :], axis=0, keepdims=True)
                    pairs = jnp.where(rank1 == float(a), n_a, pairs)
                zs = jnp.exp(cvals[0] - cvals[0])
                for k in range(1, PEER_TOPK):
                    zs = zs + jnp.exp(cvals[k] - cvals[0])
                e1_scr[hd] = jnp.exp(s1 - v1[0]) / zs
                e2_scr[hd] = jnp.exp(s2 - v2[0])
                s_scr[2 * hd] = jnp.where(pairs > 0.0, 1.0 - pairs, jnp.inf)
                s_scr[2 * hd + 1] = jnp.where(rank2 < float(PEER_TOPK), -rank2, -jnp.inf)
                return carry

            lax.fori_loop(0, PEER_HEADS, per_head, 0)

        acc_scr[...] = jnp.zeros_like(acc_scr)

    piece = 2 * nkeys
    npiece = PEER_GROUP // 2
    per_stream = npiece // PEER_STREAMS

    def pre_piece(pi):
        lo = (pi % per_stream) * piece
        return lax.dot_general(u_refs[pi // per_stream][lo:lo + piece, :], hb, _NT, preferred_element_type=F32)

    def gate_rows(i1):
        w = jnp.zeros((nkeys, tm), F32)
        for hd in range(PEER_HEADS):
            thr = s_scr[2 * hd, pl.ds(i1, 1), :]
            e1r = e1_scr[hd, pl.ds(i1, 1), :]
            w = w + jnp.where(s_scr[2 * hd + 1] >= thr, e2_scr[hd] * e1r, 0.0)
        return w

    for pi in range(npiece):
        pre = pre_piece(pi)
        i1 = ci * PEER_GROUP + 2 * pi
        w = jnp.concatenate([gate_rows(i1), gate_rows(i1 + 1)], axis=0)
        act = 0.5 * pre * (1.0 + lax.erf(pre * (2.0 ** -0.5)))
        a = (act * w).astype(BF16)
        lo = (pi % per_stream) * piece
        acc_scr[...] += jnp.dot(vt_refs[pi // per_stream][0, :, lo:lo + piece], a,
                                preferred_element_type=F32)

    @pl.when(ci == nc - 1)
    def _():
        y = acc_scr[...].T
        out_ref[...] = _layer_norm(DN_ALPHA * h_ref[...] + y, g2_ref[...], b2_ref[...])


def _peer_call(h, wqt, keys, u, vtc, g2, b2):
    n, d = h.shape
    nkeys = keys.shape[1]
    tm = min(_row_tile(n), 256)
    g = PEER_GROUP
    nc = nkeys // g
    ns = PEER_STREAMS
    rows = g * nkeys // ns
    assert vtc.shape == (nc * ns, d, rows)
    hspec = pl.BlockSpec((tm, d), lambda i, c: (i, 0))
    full = lambda a: pl.BlockSpec(a.shape, lambda i, c: (0,) * a.ndim)
    assert nkeys == PEER_TOPK * SUBLANES and PEER_HEADS == SUBLANES
    dense = pltpu.VMEM((PEER_HEADS, nkeys, tm), F32)
    u_specs = [pl.BlockSpec((rows, d), lambda i, c, k=k: (ns * c + k, 0)) for k in range(ns)]
    vt_specs = [pl.BlockSpec((1, d, rows), lambda i, c, k=k: (ns * c + k, 0, 0)) for k in range(ns)]
    return pl.pallas_call(
        functools.partial(_peer_kernel, nc=nc), grid=(n // tm, nc),
        in_specs=[hspec, full(wqt), full(keys), full(g2), full(b2)] + u_specs + vt_specs,
        out_specs=hspec, out_shape=jax.ShapeDtypeStruct((n, d), F32),
        scratch_shapes=[pltpu.VMEM((PEER_HEADS * 2 * PEER_HALF, tm), F32),
                        pltpu.VMEM((2 * PEER_HEADS, nkeys, tm), F32), dense, dense,
                        pltpu.VMEM((2, PEER_TOPK, SUBLANES, tm), F32), pltpu.VMEM((PEER_TOPK, SUBLANES, tm), F32),
                        pltpu.VMEM((4, SUBLANES, tm), F32), pltpu.VMEM((SUBLANES, tm), F32),
                        pltpu.VMEM((d, tm), F32)],
        compiler_params=_cparams(("parallel", "arbitrary")), name="peer")(
            h, wqt, keys, g2, b2, *([u] * ns), *([vtc] * ns))


def _round_up(x, m):
    return -(-x // m) * m


def _pad_rows(x, rows):
    return jnp.pad(x, ((0, 0), (0, rows - x.shape[1]), (0, 0)))


def _pad_flat(x, rows):
    return jnp.pad(x, ((0, rows - x.shape[0]), (0, 0)))


def kernel(x_prompt, x_sample, cache_k, cache_v, state_gla, meta, ln_in_g, ln_in_b, w_in, w_gk2, b_gk,
           gla_norm_g, w_pa, w_pb, w_o, ln1_g, ln1_b, peer_wq, peer_subkeys, peer_u, peer_v, ln2_g, ln2_b):
    bp, s_len, d = x_prompt.shape
    bd, t_new, _ = x_sample.shape
    past = cache_k.shape[2]
    n_len = N_META + s_len
    keep = min(s_len, past)
    lp = _round_up(n_len, ATTN_BLOCK)
    ts = _round_up(t_new, ATTN_BLOCK)
    tg = _round_up(t_new, GLA_CHUNK)
    tk_s = _round_up(N_META + past + t_new, ATTN_BLOCK)
    row = lambda a: a.reshape(1, -1)

    meta_rows = jnp.broadcast_to(meta[None].astype(x_prompt.dtype), (bp, N_META, d))
    xp = _pad_rows(jnp.concatenate([meta_rows, x_prompt], axis=1), lp).reshape(bp * lp, d)
    ns = bd * t_new
    ns_p = _round_up(ns, LANES)
    xs = _pad_flat(x_sample.reshape(ns, d), ns_p)
    hp = _ln_call(xp, ln_in_g, ln_in_b)
    hs = _ln_call(xs, ln_in_g, ln_in_b)

    outs = {k: [] for k in ("nk_p", "nv_p", "st_p", "nk_s", "nv_s", "st_s")}
    for l in range(DEPTH):
        glr0 = _SEC[7]
        w_main = jnp.concatenate([w_in[l][:, :glr0], w_in[l][:, glr0 + GLA_RANK:]], axis=1).astype(BF16)
        w_glr = jnp.pad(w_in[l][:, glr0:glr0 + GLA_RANK], ((0, 0), (0, LANES - GLA_RANK))).astype(BF16)
        w_gk2p = jnp.pad(w_gk2[l], ((0, LANES - GLA_RANK), (0, 0)))
        merge_w = (row(gla_norm_g[l]), w_pa[l].astype(BF16), w_pb[l].astype(BF16), w_o[l].astype(BF16),
                   row(ln1_g[l]), row(ln1_b[l]))
        vrows = PEER_GROUP * peer_subkeys.shape[2] // PEER_STREAMS
        vtc = jnp.swapaxes(peer_v[l].astype(BF16).reshape(-1, vrows, d), 1, 2)
        peer_w = (peer_wq[l].T.astype(BF16), peer_subkeys[l], peer_u[l].astype(BF16), vtc,
                  row(ln2_g[l]), row(ln2_b[l]))

        qa, kab, vab, ka, va, qb, kb, vb, la, rb, ga, gb = _inproj_call(hp, w_main, w_glr, w_gk2p, row(b_gk[l]))
        b3 = lambda a: a.reshape(bp, lp, a.shape[-1])
        sb = _sb_attn_call(b3(qa), b3(kab), b3(vab), 0)
        s0t = jnp.zeros((bp, GLA_HEADS, GLA_DV, GLA_DK), F32)
        go, stp = _gla_call(b3(qb), b3(kb), b3(la), b3(vb), s0t, n_len)
        hp = _merge_call(hp, sb.reshape(bp * lp, SB_W), go.reshape(bp * lp, GLA_VW), rb, ga, gb, *merge_w)
        hp = _peer_call(hp, *peer_w)
        ka3 = ka.reshape(bp, lp, SB_HEADS, SB_DIM)
        va3 = va.reshape(bp, lp, SB_HEADS, SB_DIM)
        outs["nk_p"].append(ka3[:, n_len - keep:n_len])
        outs["nv_p"].append(va3[:, n_len - keep:n_len])
        outs["st_p"].append(jnp.swapaxes(stp, 2, 3).astype(state_gla.dtype))

        qa2, _, _, ka2, va2, qb2, kb2, vb2, la2, rb2, ga2, gb2 = _inproj_call(
            hs, w_main, w_glr, w_gk2p, row(b_gk[l]))
        s3 = lambda a: a[:ns].reshape(bd, t_new, a.shape[-1])
        flat = lambda a: _pad_flat(a[:, :t_new].reshape(ns, a.shape[-1]), ns_p)
        ka2n = s3(ka2)
        va2n = s3(va2)
        meta_k = jnp.broadcast_to(ka3[:1, :N_META].reshape(1, N_META, SB_W), (bd, N_META, SB_W))
        meta_v = jnp.broadcast_to(va3[:1, :N_META].reshape(1, N_META, SB_W), (bd, N_META, SB_W))
        k_all = jnp.concatenate([meta_k, cache_k[l].reshape(bd, past, SB_W), ka2n], axis=1)
        v_all = jnp.concatenate([meta_v, cache_v[l].reshape(bd, past, SB_W), va2n], axis=1)
        sb2 = _sb_attn_call(_pad_rows(s3(qa2), ts), _pad_rows(k_all, tk_s).astype(BF16),
                            _pad_rows(v_all, tk_s).astype(BF16), N_META + past)
        g3 = lambda a: _pad_rows(s3(a), tg)
        go2, sts = _gla_call(g3(qb2), g3(kb2), g3(la2), g3(vb2), jnp.swapaxes(state_gla[l], 2, 3), t_new)
        hs = _merge_call(hs, flat(sb2), flat(go2), rb2, ga2, gb2, *merge_w)
        hs = _peer_call(hs, *peer_w)
        outs["nk_s"].append(ka2n.reshape(bd, t_new, SB_HEADS, SB_DIM))
        outs["nv_s"].append(va2n.reshape(bd, t_new, SB_HEADS, SB_DIM))
        outs["st_s"].append(jnp.swapaxes(sts, 2, 3).astype(state_gla.dtype))

    y_prompt = hp.reshape(bp, lp, d)[:, N_META:n_len]
    y_sample = hs[:ns].reshape(bd, t_new, d)
    return (y_prompt, y_sample, jnp.stack(outs["nk_p"]), jnp.stack(outs["nv_p"]), jnp.stack(outs["st_p"]),
            jnp.stack(outs["nk_s"]), jnp.stack(outs["nv_s"]), jnp.stack(outs["st_s"]))
```

```python
import functools

import jax
import jax.numpy as jnp
from jax import lax
from jax.experimental import pallas as pl
from jax.experimental.pallas import tpu as pltpu

F32 = jnp.float32
BF16 = jnp.bfloat16

D_MODEL = 1024
DEPTH = 2
N_META = 16
SB_HEADS = 8
SB_DIM = 64
SB_W = SB_HEADS * SB_DIM
GLA_HEADS = 4
GLA_DK = 128
GLA_DV = 256
GLA_KW = GLA_HEADS * GLA_DK
GLA_VW = GLA_HEADS * GLA_DV
GLA_RANK = 16
GLA_TAU = 16.0
PEER_HEADS = 8
PEER_HALF = 128
PEER_TOPK = 16
DN_ALPHA = float((2 * DEPTH) ** 0.25)
LN_EPS = 1e-5
RMS_EPS = 1e-6

LANES = 128
SUBLANES = 8
VMEM_LIMIT_BYTES = 56 * 1024 * 1024

ATTN_BLOCK = 128
GLA_CHUNK = 128
PEER_GROUP = 32
EXP_ZERO_BELOW = -104.0

_NT = (((1,), (1,)), ((), ()))
_TN = (((0,), (0,)), ((), ()))


def _cparams(sem, flags=None):
    return pltpu.CompilerParams(dimension_semantics=sem, vmem_limit_bytes=VMEM_LIMIT_BYTES, flags=flags)


def _row_tile(n):
    for t in (512, 256, 128):
        if n % t == 0:
            return t
    raise ValueError(f"token count {n} is not a multiple of 128")


def _layer_norm(x, g, b):
    mu = jnp.mean(x, axis=-1, keepdims=True)
    xc = x - mu
    var = jnp.mean(xc * xc, axis=-1, keepdims=True)
    return xc * lax.rsqrt(var + LN_EPS) * g + b


def _softplus(x):
    return jnp.maximum(x, 0.0) + jnp.log(1.0 + jnp.exp(-jnp.abs(x)))


def _sigmoid(x):
    return 1.0 / (1.0 + jnp.exp(-x))


def _ln_kernel(x_ref, g_ref, b_ref, o_ref):
    o_ref[...] = _layer_norm(x_ref[...], g_ref[...], b_ref[...])


def _ln_call(x, g, b):
    n, d = x.shape
    tm = _row_tile(n)
    row = pl.BlockSpec((tm, d), lambda i: (i, 0))
    vec = pl.BlockSpec((1, d), lambda i: (0, 0))
    return pl.pallas_call(
        _ln_kernel, grid=(n // tm,), in_specs=[row, vec, vec], out_specs=row,
        out_shape=jax.ShapeDtypeStruct((n, d), F32), compiler_params=_cparams(("parallel",)),
        name="ln_in")(x, g.reshape(1, d), b.reshape(1, d))


_SEC = (0, 512, 1024, 1536, 2048, 2560, 3584, 4608, 5632, 6656)


def _inproj_kernel(h_ref, w_ref, wg_ref, wgk2_ref, bgk_ref,
                   qa_ref, kab_ref, vab_ref, ka_ref, va_ref, qb_ref, kb_ref, vb_ref, la_ref,
                   rb_ref, ga_ref, gb_ref):
    hb = h_ref[...].astype(BF16)

    def sec(i):
        return jnp.dot(hb, w_ref[:, _SEC[i]:_SEC[i + 1]], preferred_element_type=F32)

    qa_ref[...] = (sec(0) * (SB_DIM ** -0.5)).astype(BF16)
    ka = sec(1)
    ka_ref[...] = ka
    kab_ref[...] = ka.astype(BF16)
    va = sec(2)
    va_ref[...] = va
    vab_ref[...] = va.astype(BF16)
    qb_ref[...] = sec(3) * (GLA_DK ** -0.5)
    kb_ref[...] = sec(4)
    vb_ref[...] = sec(5).astype(BF16)
    rb_ref[...] = sec(6)
    ga_ref[...] = sec(7)
    gb_ref[...] = sec(8)
    glr = jnp.dot(hb, wg_ref[...], preferred_element_type=F32)
    x = jnp.dot(glr, wgk2_ref[...], preferred_element_type=F32) + bgk_ref[...]
    la_ref[...] = -_softplus(-x) * (1.0 / GLA_TAU)


def _inproj_call(h, w_main, w_glr, w_gk2p, b_gk):
    n, d = h.shape
    tm = min(_row_tile(n), 256)
    row = lambda w: pl.BlockSpec((tm, w), lambda i: (i, 0))
    full = lambda a: pl.BlockSpec(a.shape, lambda i: (0,) * a.ndim)
    outs = [(SB_W, BF16), (SB_W, BF16), (SB_W, BF16), (SB_W, F32), (SB_W, F32),
            (GLA_KW, F32), (GLA_KW, F32), (GLA_VW, BF16), (GLA_KW, F32),
            (GLA_VW, F32), (D_MODEL, F32), (D_MODEL, F32)]
    return pl.pallas_call(
        _inproj_kernel, grid=(n // tm,),
        in_specs=[row(d), full(w_main), full(w_glr), full(w_gk2p), full(b_gk)],
        out_specs=[row(w) for w, _ in outs],
        out_shape=[jax.ShapeDtypeStruct((n, w), t) for w, t in outs],
        compiler_params=_cparams(("parallel",)), name="in_proj")(h, w_main, w_glr, w_gk2p, b_gk)


def _sb_attn_kernel(q_ref, k_ref, v_ref, o_ref, carry_scr, acc_scr, *, q_off, nk):
    tq = tk = ATTN_BLOCK
    npair = SB_W // LANES
    i = pl.program_id(1)
    lane = lax.broadcasted_iota(jnp.int32, (tq, LANES), 1)
    r_iota = lax.broadcasted_iota(jnp.int32, (tq, tk), 0)
    c_iota = lax.broadcasted_iota(jnp.int32, (tq, tk), 1)
    later_ones = jnp.concatenate([(r_iota > c_iota).astype(BF16), jnp.ones((tk, tk), BF16)], axis=1)
    qpos = q_off + i * tq + r_iota
    j_start = jnp.minimum((q_off + i * tq + tq - 2) // tk, nk - 1)
    carry_scr[...] = jnp.zeros_like(carry_scr)
    acc_scr[...] = jnp.zeros_like(acc_scr)

    def cond(c):
        j, cmax = c
        return jnp.logical_and(j >= 0, cmax > EXP_ZERO_BELOW)

    def body(c):
        j, _ = c
        off = pl.multiple_of(j * tk, tk)
        mask = (off + c_iota) < qpos
        heads = range(SB_HEADS)
        zs = []
        for h in heads:
            cols = slice((h // 2) * LANES, (h // 2 + 1) * LANES)
            q = q_ref[0, :, cols]
            qm = jnp.where(lane < SB_DIM if h % 2 == 0 else lane >= SB_DIM, q, jnp.zeros_like(q))
            zs.append(lax.dot_general(qm, k_ref[0, pl.ds(off, tk), cols], _NT, preferred_element_type=F32))
        lbs, css = [], []
        for h in heads:
            sp = _softplus(zs[h])
            lbs.append(zs[h] - sp)
            lk = jnp.where(mask, -sp, 0.0)
            hi = lk.astype(BF16)
            lo = (lk - hi.astype(F32)).astype(BF16)
            css.append(jnp.dot(hi, later_ones, preferred_element_type=F32)
                       + jnp.dot(lo, later_ones, preferred_element_type=F32))
        cmax = None
        pvs = []
        for h in heads:
            cols = slice((h // 2) * LANES, (h // 2 + 1) * LANES)
            carry = carry_scr[h]
            w = jnp.where(mask, jnp.exp(lbs[h] + css[h][:, :tk] + carry), 0.0).astype(BF16)
            pvs.append(jnp.dot(w, v_ref[0, pl.ds(off, tk), cols], preferred_element_type=F32))
            carry = carry + css[h][:, tk:]
            carry_scr[h] = carry
            cmax = carry if cmax is None else jnp.maximum(cmax, carry)
        for p in range(npair):
            acc_scr[p] += jnp.where(lane < SB_DIM, pvs[2 * p], pvs[2 * p + 1])
        return j - 1, jnp.max(cmax)

    lax.while_loop(cond, body, (j_start, jnp.float32(0.0)))
    for p in range(npair):
        o_ref[0, :, p * LANES:(p + 1) * LANES] = acc_scr[p]


def _sb_attn_call(q, k, v, q_off):
    b, tq_all, _ = q.shape
    tk_all = k.shape[1]
    nq, nk = tq_all // ATTN_BLOCK, tk_all // ATTN_BLOCK
    qspec = pl.BlockSpec((1, ATTN_BLOCK, SB_W), lambda bi, i: (bi, i, 0))
    kspec = pl.BlockSpec((1, tk_all, SB_W), lambda bi, i: (bi, 0, 0))
    return pl.pallas_call(
        functools.partial(_sb_attn_kernel, q_off=q_off, nk=nk),
        grid=(b, nq), in_specs=[qspec, kspec, kspec], out_specs=qspec,
        out_shape=jax.ShapeDtypeStruct((b, tq_all, SB_W), F32),
        scratch_shapes=[pltpu.VMEM((SB_HEADS, ATTN_BLOCK, ATTN_BLOCK), F32),
                        pltpu.VMEM((SB_W // LANES, ATTN_BLOCK, LANES), F32)],
        compiler_params=_cparams(("parallel", "arbitrary")), name="sb_attn")(q, k, v)


def _half_sizes(c):
    out, m = [], c // 2
    while m >= 1:
        out.append(m)
        m //= 2
    return out


def _block_ref_rows(b, m):
    c, dk = b.shape
    if 2 * m >= SUBLANES:
        pieces = [jnp.broadcast_to(b[a + m - 1:a + m, :], (2 * m, dk)) for a in range(0, c, 2 * m)]
        return pieces[0] if len(pieces) == 1 else jnp.concatenate(pieces, axis=0)
    row8 = lax.broadcasted_iota(jnp.int32, (SUBLANES, dk), 0)
    tiles = []
    for a in range(0, c, SUBLANES):
        tile = None
        for off in range(0, SUBLANES, 2 * m):
            r = jnp.broadcast_to(b[a + off + m - 1:a + off + m, :], (SUBLANES, dk))
            tile = r if tile is None else jnp.where(row8 >= off, r, tile)
        tiles.append(tile)
    return jnp.concatenate(tiles, axis=0)


def _gla_kernel(q_ref, k_ref, la_ref, v_ref, s0_ref, o_ref, st_ref, *, n_valid):
    c = GLA_CHUNK
    ci = pl.program_id(1)

    @pl.when(ci == 0)
    def _():
        st_ref[...] = s0_ref[...]

    r_iota = lax.broadcasted_iota(jnp.int32, (c, c), 0)
    c_iota = lax.broadcasted_iota(jnp.int32, (c, c), 1)
    tril = (r_iota >= c_iota).astype(BF16)
    level = jnp.where(c_iota < r_iota, 31 - lax.clz(r_iota ^ c_iota), jnp.where(c_iota == r_iota, -1, -2))
    valid = (ci * c + lax.broadcasted_iota(jnp.int32, (c, GLA_DK), 0)) < n_valid

    for h in range(GLA_HEADS):
        ks = slice(h * GLA_DK, (h + 1) * GLA_DK)
        vs = slice(h * GLA_DV, (h + 1) * GLA_DV)
        la = jnp.where(valid, la_ref[0, :, ks], 0.0)
        k = jnp.where(valid, k_ref[0, :, ks], 0.0)
        q = q_ref[0, :, ks]
        v = v_ref[0, :, vs]
        l1 = la.astype(BF16)
        r1 = la - l1.astype(F32)
        l2 = r1.astype(BF16)
        l3 = (r1 - l2.astype(F32)).astype(BF16)
        b = (jnp.dot(tril, l1, preferred_element_type=F32) + jnp.dot(tril, l2, preferred_element_type=F32)
             + jnp.dot(tril, l3, preferred_element_type=F32))
        b_last = b[c - 1:c, :]
        st = st_ref[0, h]
        inter = lax.dot_general((q * jnp.exp(b)).astype(BF16), st.astype(BF16), _NT,
                                preferred_element_type=F32)
        scores = jnp.where(level == -1, lax.dot_general(q.astype(BF16), k.astype(BF16), _NT,
                                                        preferred_element_type=F32), 0.0)
        for m in _half_sizes(c):
            ref_rows = _block_ref_rows(b, m)
            qm = (q * jnp.exp(jnp.minimum(b - ref_rows, 0.0))).astype(BF16)
            km = (k * jnp.exp(jnp.minimum(ref_rows - b, 0.0))).astype(BF16)
            scores = jnp.where(level == m.bit_length() - 1,
                               lax.dot_general(qm, km, _NT, preferred_element_type=F32), scores)
        intra = jnp.dot(scores.astype(BF16), v, preferred_element_type=F32)
        o_ref[0, :, vs] = inter + intra
        kd = (k * jnp.exp(b_last - b)).astype(BF16)
        st_ref[0, h] = st * jnp.exp(b_last) + lax.dot_general(v, kd, _TN, preferred_element_type=F32)


def _gla_call(q, k, la, v, s0t, n_valid):
    b, t, _ = q.shape
    c = GLA_CHUNK
    kspec = pl.BlockSpec((1, c, GLA_KW), lambda bi, ci: (bi, ci, 0))
    vspec = pl.BlockSpec((1, c, GLA_VW), lambda bi, ci: (bi, ci, 0))
    sspec = pl.BlockSpec((1, GLA_HEADS, GLA_DV, GLA_DK), lambda bi, ci: (bi, 0, 0, 0))
    return pl.pallas_call(
        functools.partial(_gla_kernel, n_valid=n_valid),
        grid=(b, t // c), in_specs=[kspec, kspec, kspec, vspec, sspec],
        out_specs=[vspec, sspec],
        out_shape=[jax.ShapeDtypeStruct((b, t, GLA_VW), F32),
                   jax.ShapeDtypeStruct((b, GLA_HEADS, GLA_DV, GLA_DK), F32)],
        compiler_params=_cparams(("parallel", "arbitrary")), name="gla")(q, k, la, v, s0t)


def _merge_kernel(h_ref, sb_ref, go_ref, rb_ref, ga_ref, gb_ref, gn_ref, wpa_ref, wpb_ref, wo_ref,
                  g1_ref, b1_ref, out_ref):
    go = go_ref[...]
    parts = []
    for h in range(GLA_HEADS):
        x = go[:, h * GLA_DV:(h + 1) * GLA_DV]
        ms = jnp.mean(x * x, axis=-1, keepdims=True)
        parts.append(x * lax.rsqrt(ms + RMS_EPS) * gn_ref[...])
    o = jnp.concatenate(parts, axis=-1)
    rb = rb_ref[...]
    gated = (o * (rb * _sigmoid(rb))).astype(BF16)
    gla_y = jnp.dot(gated, wpb_ref[...], preferred_element_type=F32)
    sb_y = jnp.dot(sb_ref[...].astype(BF16), wpa_ref[...], preferred_element_type=F32)
    merged = _sigmoid(ga_ref[...]) * sb_y + _sigmoid(gb_ref[...]) * gla_y
    y = jnp.dot(merged.astype(BF16), wo_ref[...], preferred_element_type=F32)
    out_ref[...] = _layer_norm(DN_ALPHA * h_ref[...] + y, g1_ref[...], b1_ref[...])


def _merge_call(h, sb, go, rb, ga, gb, gn, wpa, wpb, wo, g1, b1):
    n, d = h.shape
    tm = min(_row_tile(n), 256)
    row = lambda w: pl.BlockSpec((tm, w), lambda i: (i, 0))
    full = lambda a: pl.BlockSpec(a.shape, lambda i: (0,) * a.ndim)
    args = (h, sb, go, rb, ga, gb, gn, wpa, wpb, wo, g1, b1)
    return pl.pallas_call(
        _merge_kernel, grid=(n // tm,),
        in_specs=[row(d), row(SB_W), row(GLA_VW), row(GLA_VW), row(d), row(d)] + [full(a) for a in args[6:]],
        out_specs=row(d), out_shape=jax.ShapeDtypeStruct((n, d), F32),
        compiler_params=_cparams(("parallel",)), name="merge")(*args)


def _sort_network(n):
    def merge(lo, hi, r):
        step = r * 2
        if step < hi - lo:
            yield from merge(lo, hi, step)
            yield from merge(lo + r, hi, step)
            yield from [(i, i + r) for i in range(lo + r, hi - r, step)]
        else:
            yield (lo, lo + r)

    def sort(lo, hi):
        if hi - lo >= 1:
            mid = lo + (hi - lo) // 2
            yield from sort(lo, mid)
            yield from sort(mid + 1, hi)
            yield from merge(lo, hi, 1)

    return tuple(sort(0, n - 1))


def _bitonic_merge_network(n):
    out, k = [], n // 2
    while k >= 1:
        out.extend((i, i + k) for i in range(n) if not i & k)
        k //= 2
    return tuple(out)


_SORT_TOPK = _sort_network(PEER_TOPK)
_MERGE_TOPK = _bitonic_merge_network(PEER_TOPK)


def _exchange(v, net):
    v = list(v)
    for i, j in net:
        hi, lo = jnp.maximum(v[i], v[j]), jnp.minimum(v[i], v[j])
        v[i], v[j] = hi, lo
    return v


def _merge_top(x, y):
    n = PEER_TOPK
    return _exchange([jnp.maximum(x[k], y[n - 1 - k]) for k in range(n)], _MERGE_TOPK)


def _column_top(s):
    v = _exchange([s[SUBLANES * k:SUBLANES * (k + 1), :] for k in range(PEER_TOPK)], _SORT_TOPK)
    for shift in (4, 2, 1):
        v = _merge_top(v, [pltpu.roll(x, shift, 0) for x in v])
    return v


_CAND_COLS = tuple(PEER_TOPK // (a + 1) for a in range(PEER_TOPK // 2))


def _ranked_top(s):
    rows = s.shape[0]
    iota = lax.broadcasted_iota(jnp.int32, s.shape, 0)
    rank = jnp.full(s.shape, float(PEER_TOPK), F32)
    vals = []
    for k in range(PEER_TOPK):
        m = jnp.max(s, axis=0, keepdims=True)
        first = jnp.min(jnp.where(s == m, iota, rows), axis=0, keepdims=True)
        hit = iota == first
        rank = jnp.where(hit, float(k), rank)
        s = jnp.where(hit, -jnp.inf, s)
        vals.append(m)
    return rank, vals


def _peer_kernel(h_ref, wqt_ref, keys_ref, u_ref, vt_ref, g2_ref, b2_ref, out_ref,
                 qt_scr, s_scr, e1_scr, e2_scr, top_scr, th_scr, stat_scr, flag_scr, acc_scr, hbt_scr, *, nc):
    ci = pl.program_id(1)
    tm = h_ref.shape[0]
    nkeys = keys_ref.shape[1]

    @pl.when(ci == 0)
    def _():
        hbt_scr[...] = h_ref[...].T.astype(BF16)
        qt_scr[...] = jnp.dot(wqt_ref[...], hbt_scr[...], preferred_element_type=F32)
        top_scr[...] = jnp.zeros_like(top_scr)
        flag_scr[...] = jnp.zeros_like(flag_scr)
        sub = lax.broadcasted_iota(jnp.int32, (SUBLANES, tm), 0)

        def per_half(hs, carry):
            hd = lax.shift_right_logical(hs, 1)
            side = hs & 1
            qh = qt_scr[pl.ds(pl.multiple_of(hs * PEER_HALF, PEER_HALF), PEER_HALF), :]
            s = jnp.dot(keys_ref[side], qh, preferred_element_type=F32)
            s_scr[hs] = s
            top = _column_top(s)
            at_least = jnp.sum(jnp.where(s >= top[PEER_TOPK - 1][0:1, :], 1.0, 0.0), axis=0, keepdims=True)
            tie = jnp.where(at_least != float(PEER_TOPK), 1.0, flag_scr[...])
            for k in range(PEER_TOPK):
                top_scr[side, k] = jnp.where(sub == hd, top[k], top_scr[side, k])
                if k:
                    tie = jnp.where(top[k - 1] == top[k], 1.0, tie)
            flag_scr[...] = tie
            return carry

        lax.fori_loop(0, 2 * PEER_HEADS, per_half, 0)

        t1 = [top_scr[0, k] for k in range(PEER_TOPK)]
        t2 = [top_scr[1, k] for k in range(PEER_TOPK)]
        neg = jnp.full((SUBLANES, tm), -jnp.inf, F32)
        lists = [[t1[a] + t2[b] for b in range(nb)] + [neg] * (PEER_TOPK - nb) for a, nb in enumerate(_CAND_COLS)]
        half = PEER_TOPK // 2
        lists.append([t1[a] + t2[0] for a in range(half, PEER_TOPK)] + [neg] * half)
        while len(lists) > 1:
            lists = [_merge_top(lists[i], lists[i + 1]) if i + 1 < len(lists) else lists[i]
                     for i in range(0, len(lists), 2)]
        cv = lists[0]
        tau = cv[PEER_TOPK - 1]
        zsum = jnp.exp(cv[0] - cv[0])
        for k in range(1, PEER_TOPK):
            zsum = zsum + jnp.exp(cv[k] - cv[0])
        stat_scr[1] = zsum
        stat_scr[2] = t1[0]
        stat_scr[3] = t2[0]
        n_reach = jnp.zeros((SUBLANES, tm), F32)
        for a in range(PEER_TOPK):
            th = jnp.full((SUBLANES, tm), jnp.inf, F32)
            for b in range(PEER_TOPK // (a + 1)):
                reach = t1[a] + t2[b] >= tau
                th = jnp.where(reach, t2[b], th)
                n_reach = n_reach + jnp.where(reach, 1.0, 0.0)
            th_scr[a] = th
        any_tie = jnp.max(jnp.where(n_reach != float(PEER_TOPK), 1.0, flag_scr[...])) > 0.0

        @pl.when(jnp.logical_not(any_tie))
        def _():
            def per_head(hd, carry):
                zr = stat_scr[1, pl.ds(hd, 1), :]
                m1 = stat_scr[2, pl.ds(hd, 1), :]
                m2 = stat_scr[3, pl.ds(hd, 1), :]
                s1 = s_scr[2 * hd]
                e1_scr[hd] = jnp.exp(s1 - m1) / zr
                e2_scr[hd] = jnp.exp(s_scr[2 * hd + 1] - m2)
                thr = jnp.full((nkeys, tm), jnp.inf, F32)
                for a in range(PEER_TOPK):
                    thr = jnp.where(s1 == top_scr[0, a, pl.ds(hd, 1), :], th_scr[a, pl.ds(hd, 1), :], thr)
                s_scr[2 * hd] = thr
                return carry

            lax.fori_loop(0, PEER_HEADS, per_head, 0)

        @pl.when(any_tie)
        def _():
            def per_head(hd, carry):
                s1 = s_scr[2 * hd]
                s2 = s_scr[2 * hd + 1]
                rank1, v1 = _ranked_top(s1)
                rank2, v2 = _ranked_top(s2)
                v2col = jnp.concatenate(v2, axis=0)
                cand = jnp.concatenate([v1[a] + v2col for a in range(PEER_TOPK)], axis=0)
                crank, cvals = _ranked_top(cand)
                taken = jnp.where(crank < float(PEER_TOPK), 1.0, 0.0)
                pairs = jnp.zeros((nkeys, tm), F32)
                for a in range(PEER_TOPK):
                    n_a = jnp.sum(taken[a * PEER_TOPK:(a + 1) * PEER_TOPK, :], axis=0, keepdims=True)
                    pairs = jnp.where(rank1 == float(a), n_a, pairs)
                zs = jnp.exp(cvals[0] - cvals[0])
                for k in range(1, PEER_TOPK):
                    zs = zs + jnp.exp(cvals[k] - cvals[0])
                e1_scr[hd] = jnp.exp(s1 - v1[0]) / zs
                e2_scr[hd] = jnp.exp(s2 - v2[0])
                s_scr[2 * hd] = jnp.where(pairs > 0.0, 1.0 - pairs, jnp.inf)
                s_scr[2 * hd + 1] = jnp.where(rank2 < float(PEER_TOPK), -rank2, -jnp.inf)
                return carry

            lax.fori_loop(0, PEER_HEADS, per_head, 0)

        acc_scr[...] = jnp.zeros_like(acc_scr)

    piece = 2 * nkeys
    npiece = PEER_GROUP // 2

    def pre_piece(pi):
        return jnp.dot(u_ref[pi * piece:(pi + 1) * piece, :], hbt_scr[...], preferred_element_type=F32)

    def gate_rows(i1):
        w = jnp.zeros((nkeys, tm), F32)
        for hd in range(PEER_HEADS):
            thr = s_scr[2 * hd, pl.ds(i1, 1), :]
            e1r = e1_scr[hd, pl.ds(i1, 1), :]
            w = w + jnp.where(s_scr[2 * hd + 1] >= thr, e2_scr[hd] * e1r, 0.0)
        return w

    for pi in range(npiece):
        pre = pre_piece(pi)
        i1 = ci * PEER_GROUP + 2 * pi
        w = jnp.concatenate([gate_rows(i1), gate_rows(i1 + 1)], axis=0)
        act = 0.5 * pre * (1.0 + lax.erf(pre * (2.0 ** -0.5)))
        a = (act * w).astype(BF16)
        acc_scr[...] += jnp.dot(vt_ref[:, pi * piece:(pi + 1) * piece], a, preferred_element_type=F32)

    @pl.when(ci == nc - 1)
    def _():
        y = acc_scr[...].T
        out_ref[...] = _layer_norm(DN_ALPHA * h_ref[...] + y, g2_ref[...], b2_ref[...])


def _peer_call(h, wqt, keys, u, vt, g2, b2):
    n, d = h.shape
    nkeys = keys.shape[1]
    tm = min(_row_tile(n), 256)
    g = PEER_GROUP
    nc = nkeys // g
    hspec = pl.BlockSpec((tm, d), lambda i, c: (i, 0))
    full = lambda a: pl.BlockSpec(a.shape, lambda i, c: (0,) * a.ndim)
    assert nkeys == PEER_TOPK * SUBLANES and PEER_HEADS == SUBLANES
    dense = pltpu.VMEM((PEER_HEADS, nkeys, tm), F32)
    return pl.pallas_call(
        functools.partial(_peer_kernel, nc=nc), grid=(n // tm, nc),
        in_specs=[hspec, full(wqt), full(keys),
                  pl.BlockSpec((g * nkeys, d), lambda i, c: (c, 0)),
                  pl.BlockSpec((d, g * nkeys), lambda i, c: (0, c)), full(g2), full(b2)],
        out_specs=hspec, out_shape=jax.ShapeDtypeStruct((n, d), F32),
        scratch_shapes=[pltpu.VMEM((PEER_HEADS * 2 * PEER_HALF, tm), F32),
                        pltpu.VMEM((2 * PEER_HEADS, nkeys, tm), F32), dense, dense,
                        pltpu.VMEM((2, PEER_TOPK, SUBLANES, tm), F32), pltpu.VMEM((PEER_TOPK, SUBLANES, tm), F32),
                        pltpu.VMEM((4, SUBLANES, tm), F32), pltpu.VMEM((SUBLANES, tm), F32),
                        pltpu.VMEM((d, tm), F32), pltpu.VMEM((d, tm), BF16)],
        compiler_params=_cparams(("parallel", "arbitrary")), name="peer")(h, wqt, keys, u, vt, g2, b2)


def _round_up(x, m):
    return -(-x // m) * m


def _pad_rows(x, rows):
    return jnp.pad(x, ((0, 0), (0, rows - x.shape[1]), (0, 0)))


def _pad_flat(x, rows):
    return jnp.pad(x, ((0, rows - x.shape[0]), (0, 0)))


def kernel(x_prompt, x_sample, cache_k, cache_v, state_gla, meta, ln_in_g, ln_in_b, w_in, w_gk2, b_gk,
           gla_norm_g, w_pa, w_pb, w_o, ln1_g, ln1_b, peer_wq, peer_subkeys, peer_u, peer_v, ln2_g, ln2_b):
    bp, s_len, d = x_prompt.shape
    bd, t_new, _ = x_sample.shape
    past = cache_k.shape[2]
    n_len = N_META + s_len
    keep = min(s_len, past)
    lp = _round_up(n_len, ATTN_BLOCK)
    ts = _round_up(t_new, ATTN_BLOCK)
    tg = _round_up(t_new, GLA_CHUNK)
    tk_s = _round_up(N_META + past + t_new, ATTN_BLOCK)
    row = lambda a: a.reshape(1, -1)

    meta_rows = jnp.broadcast_to(meta[None].astype(x_prompt.dtype), (bp, N_META, d))
    xp = _pad_rows(jnp.concatenate([meta_rows, x_prompt], axis=1), lp).reshape(bp * lp, d)
    ns = bd * t_new
    ns_p = _round_up(ns, LANES)
    xs = _pad_flat(x_sample.reshape(ns, d), ns_p)
    hp = _ln_call(xp, ln_in_g, ln_in_b)
    hs = _ln_call(xs, ln_in_g, ln_in_b)

    outs = {k: [] for k in ("nk_p", "nv_p", "st_p", "nk_s", "nv_s", "st_s")}
    for l in range(DEPTH):
        glr0 = _SEC[7]
        w_main = jnp.concatenate([w_in[l][:, :glr0], w_in[l][:, glr0 + GLA_RANK:]], axis=1).astype(BF16)
        w_glr = jnp.pad(w_in[l][:, glr0:glr0 + GLA_RANK], ((0, 0), (0, LANES - GLA_RANK))).astype(BF16)
        w_gk2p = jnp.pad(w_gk2[l], ((0, LANES - GLA_RANK), (0, 0)))
        merge_w = (row(gla_norm_g[l]), w_pa[l].astype(BF16), w_pb[l].astype(BF16), w_o[l].astype(BF16),
                   row(ln1_g[l]), row(ln1_b[l]))
        peer_w = (peer_wq[l].T.astype(BF16), peer_subkeys[l], peer_u[l].astype(BF16),
                  peer_v[l].T.astype(BF16), row(ln2_g[l]), row(ln2_b[l]))

        qa, kab, vab, ka, va, qb, kb, vb, la, rb, ga, gb = _inproj_call(hp, w_main, w_glr, w_gk2p, row(b_gk[l]))
        b3 = lambda a: a.reshape(bp, lp, a.shape[-1])
        sb = _sb_attn_call(b3(qa), b3(kab), b3(vab), 0)
        s0t = jnp.zeros((bp, GLA_HEADS, GLA_DV, GLA_DK), F32)
        go, stp = _gla_call(b3(qb), b3(kb), b3(la), b3(vb), s0t, n_len)
        hp = _merge_call(hp, sb.reshape(bp * lp, SB_W), go.reshape(bp * lp, GLA_VW), rb, ga, gb, *merge_w)
        hp = _peer_call(hp, *peer_w)
        ka3 = ka.reshape(bp, lp, SB_HEADS, SB_DIM)
        va3 = va.reshape(bp, lp, SB_HEADS, SB_DIM)
        outs["nk_p"].append(ka3[:, n_len - keep:n_len])
        outs["nv_p"].append(va3[:, n_len - keep:n_len])
        outs["st_p"].append(jnp.swapaxes(stp, 2, 3).astype(state_gla.dtype))

        qa2, _, _, ka2, va2, qb2, kb2, vb2, la2, rb2, ga2, gb2 = _inproj_call(
            hs, w_main, w_glr, w_gk2p, row(b_gk[l]))
        s3 = lambda a: a[:ns].reshape(bd, t_new, a.shape[-1])
        flat = lambda a: _pad_flat(a[:, :t_new].reshape(ns, a.shape[-1]), ns_p)
        ka2n = s3(ka2)
        va2n = s3(va2)
        meta_k = jnp.broadcast_to(ka3[:1, :N_META].reshape(1, N_META, SB_W), (bd, N_META, SB_W))
        meta_v = jnp.broadcast_to(va3[:1, :N_META].reshape(1, N_META, SB_W), (bd, N_META, SB_W))
        k_all = jnp.concatenate([meta_k, cache_k[l].reshape(bd, past, SB_W), ka2n], axis=1)
        v_all = jnp.concatenate([meta_v, cache_v[l].reshape(bd, past, SB_W), va2n], axis=1)
        sb2 = _sb_attn_call(_pad_rows(s3(qa2), ts), _pad_rows(k_all, tk_s).astype(BF16),
                            _pad_rows(v_all, tk_s).astype(BF16), N_META + past)
        g3 = lambda a: _pad_rows(s3(a), tg)
        go2, sts = _gla_call(g3(qb2), g3(kb2), g3(la2), g3(vb2), jnp.swapaxes(state_gla[l], 2, 3), t_new)
        hs = _merge_call(hs, flat(sb2), flat(go2), rb2, ga2, gb2, *merge_w)
        hs = _peer_call(hs, *peer_w)
        outs["nk_s"].append(ka2n.reshape(bd, t_new, SB_HEADS, SB_DIM))
        outs["nv_s"].append(va2n.reshape(bd, t_new, SB_HEADS, SB_DIM))
        outs["st_s"].append(jnp.swapaxes(sts, 2, 3).astype(state_gla.dtype))

    y_prompt = hp.reshape(bp, lp, d)[:, N_META:n_len]
    y_sample = hs[:ns].reshape(bd, t_new, d)
    return (y_prompt, y_sample, jnp.stack(outs["nk_p"]), jnp.stack(outs["nv_p"]), jnp.stack(outs["st_p"]),
            jnp.stack(outs["nk_s"]), jnp.stack(outs["nv_s"]), jnp.stack(outs["st_s"]))
```

```python
import functools

import jax
import jax.numpy as jnp
from jax import lax
from jax.experimental import pallas as pl
from jax.experimental.pallas import tpu as pltpu

F32 = jnp.float32
BF16 = jnp.bfloat16

D_MODEL = 1024
DEPTH = 2
N_META = 16
SB_HEADS = 8
SB_DIM = 64
SB_W = SB_HEADS * SB_DIM
GLA_HEADS = 4
GLA_DK = 128
GLA_DV = 256
GLA_KW = GLA_HEADS * GLA_DK
GLA_VW = GLA_HEADS * GLA_DV
GLA_RANK = 16
GLA_TAU = 16.0
PEER_HEADS = 8
PEER_HALF = 128
PEER_TOPK = 16
DN_ALPHA = float((2 * DEPTH) ** 0.25)
LN_EPS = 1e-5
RMS_EPS = 1e-6

LANES = 128
SUBLANES = 8
VMEM_LIMIT_BYTES = 56 * 1024 * 1024

ATTN_BLOCK = 128
GLA_CHUNK = 128
PEER_GROUP = 16
PEER_SUBTILE = 256
EXP_ZERO_BELOW = -104.0

_NT = (((1,), (1,)), ((), ()))
_TN = (((0,), (0,)), ((), ()))


def _cparams(sem, flags=None):
    return pltpu.CompilerParams(dimension_semantics=sem, vmem_limit_bytes=VMEM_LIMIT_BYTES, flags=flags)


def _row_tile(n):
    for t in (512, 256, 128):
        if n % t == 0:
            return t
    raise ValueError(f"token count {n} is not a multiple of 128")


def _layer_norm(x, g, b):
    mu = jnp.mean(x, axis=-1, keepdims=True)
    xc = x - mu
    var = jnp.mean(xc * xc, axis=-1, keepdims=True)
    return xc * lax.rsqrt(var + LN_EPS) * g + b


def _softplus(x):
    return jnp.maximum(x, 0.0) + jnp.log(1.0 + jnp.exp(-jnp.abs(x)))


def _sigmoid(x):
    return 1.0 / (1.0 + jnp.exp(-x))


def _ln_kernel(x_ref, g_ref, b_ref, o_ref):
    o_ref[...] = _layer_norm(x_ref[...], g_ref[...], b_ref[...])


def _ln_call(x, g, b):
    n, d = x.shape
    tm = _row_tile(n)
    row = pl.BlockSpec((tm, d), lambda i: (i, 0))
    vec = pl.BlockSpec((1, d), lambda i: (0, 0))
    return pl.pallas_call(
        _ln_kernel, grid=(n // tm,), in_specs=[row, vec, vec], out_specs=row,
        out_shape=jax.ShapeDtypeStruct((n, d), F32), compiler_params=_cparams(("parallel",)),
        name="ln_in")(x, g.reshape(1, d), b.reshape(1, d))


_SEC = (0, 512, 1024, 1536, 2048, 2560, 3584, 4608, 5632, 6656)


def _inproj_kernel(h_ref, w_ref, wg_ref, wgk2_ref, bgk_ref,
                   qa_ref, kab_ref, vab_ref, ka_ref, va_ref, qb_ref, kb_ref, vb_ref, la_ref,
                   rb_ref, ga_ref, gb_ref):
    hb = h_ref[...].astype(BF16)

    def sec(i):
        return jnp.dot(hb, w_ref[:, _SEC[i]:_SEC[i + 1]], preferred_element_type=F32)

    qa_ref[...] = (sec(0) * (SB_DIM ** -0.5)).astype(BF16)
    ka = sec(1)
    ka_ref[...] = ka
    kab_ref[...] = ka.astype(BF16)
    va = sec(2)
    va_ref[...] = va
    vab_ref[...] = va.astype(BF16)
    qb_ref[...] = sec(3) * (GLA_DK ** -0.5)
    kb_ref[...] = sec(4)
    vb_ref[...] = sec(5).astype(BF16)
    rb_ref[...] = sec(6)
    ga_ref[...] = sec(7)
    gb_ref[...] = sec(8)
    glr = jnp.dot(hb, wg_ref[...], preferred_element_type=F32)
    x = jnp.dot(glr, wgk2_ref[...], preferred_element_type=F32) + bgk_ref[...]
    la_ref[...] = -_softplus(-x) * (1.0 / GLA_TAU)


def _inproj_call(h, w_main, w_glr, w_gk2p, b_gk):
    n, d = h.shape
    tm = min(_row_tile(n), 256)
    row = lambda w: pl.BlockSpec((tm, w), lambda i: (i, 0))
    full = lambda a: pl.BlockSpec(a.shape, lambda i: (0,) * a.ndim)
    outs = [(SB_W, BF16), (SB_W, BF16), (SB_W, BF16), (SB_W, F32), (SB_W, F32),
            (GLA_KW, F32), (GLA_KW, F32), (GLA_VW, BF16), (GLA_KW, F32),
            (GLA_VW, F32), (D_MODEL, F32), (D_MODEL, F32)]
    return pl.pallas_call(
        _inproj_kernel, grid=(n // tm,),
        in_specs=[row(d), full(w_main), full(w_glr), full(w_gk2p), full(b_gk)],
        out_specs=[row(w) for w, _ in outs],
        out_shape=[jax.ShapeDtypeStruct((n, w), t) for w, t in outs],
        compiler_params=_cparams(("parallel",)), name="in_proj")(h, w_main, w_glr, w_gk2p, b_gk)


def _sb_attn_kernel(q_ref, k_ref, v_ref, o_ref, carry_scr, acc_scr, *, q_off, nk):
    tq = tk = ATTN_BLOCK
    npair = SB_W // LANES
    i = pl.program_id(1)
    lane = lax.broadcasted_iota(jnp.int32, (tq, LANES), 1)
    r_iota = lax.broadcasted_iota(jnp.int32, (tq, tk), 0)
    c_iota = lax.broadcasted_iota(jnp.int32, (tq, tk), 1)
    later_ones = jnp.concatenate([(r_iota > c_iota).astype(BF16), jnp.ones((tk, tk), BF16)], axis=1)
    qpos = q_off + i * tq + r_iota
    j_start = jnp.minimum((q_off + i * tq + tq - 2) // tk, nk - 1)
    carry_scr[...] = jnp.zeros_like(carry_scr)
    acc_scr[...] = jnp.zeros_like(acc_scr)

    def cond(c):
        j, cmax = c
        return jnp.logical_and(j >= 0, cmax > EXP_ZERO_BELOW)

    def body(c):
        j, _ = c
        off = pl.multiple_of(j * tk, tk)
        mask = (off + c_iota) < qpos
        heads = range(SB_HEADS)
        zs = []
        for h in heads:
            cols = slice((h // 2) * LANES, (h // 2 + 1) * LANES)
            q = q_ref[0, :, cols]
            qm = jnp.where(lane < SB_DIM if h % 2 == 0 else lane >= SB_DIM, q, jnp.zeros_like(q))
            zs.append(lax.dot_general(qm, k_ref[0, pl.ds(off, tk), cols], _NT, preferred_element_type=F32))
        lbs, css = [], []
        for h in heads:
            sp = _softplus(zs[h])
            lbs.append(zs[h] - sp)
            lk = jnp.where(mask, -sp, 0.0)
            hi = lk.astype(BF16)
            lo = (lk - hi.astype(F32)).astype(BF16)
            css.append(jnp.dot(hi, later_ones, preferred_element_type=F32)
                       + jnp.dot(lo, later_ones, preferred_element_type=F32))
        cmax = None
        pvs = []
        for h in heads:
            cols = slice((h // 2) * LANES, (h // 2 + 1) * LANES)
            carry = carry_scr[h]
            w = jnp.where(mask, jnp.exp(lbs[h] + css[h][:, :tk] + carry), 0.0).astype(BF16)
            pvs.append(jnp.dot(w, v_ref[0, pl.ds(off, tk), cols], preferred_element_type=F32))
            carry = carry + css[h][:, tk:]
            carry_scr[h] = carry
            cmax = carry if cmax is None else jnp.maximum(cmax, carry)
        for p in range(npair):
            acc_scr[p] += jnp.where(lane < SB_DIM, pvs[2 * p], pvs[2 * p + 1])
        return j - 1, jnp.max(cmax)

    lax.while_loop(cond, body, (j_start, jnp.float32(0.0)))
    for p in range(npair):
        o_ref[0, :, p * LANES:(p + 1) * LANES] = acc_scr[p]


def _sb_attn_call(q, k, v, q_off):
    b, tq_all, _ = q.shape
    tk_all = k.shape[1]
    nq, nk = tq_all // ATTN_BLOCK, tk_all // ATTN_BLOCK
    qspec = pl.BlockSpec((1, ATTN_BLOCK, SB_W), lambda bi, i: (bi, i, 0))
    kspec = pl.BlockSpec((1, tk_all, SB_W), lambda bi, i: (bi, 0, 0))
    return pl.pallas_call(
        functools.partial(_sb_attn_kernel, q_off=q_off, nk=nk),
        grid=(b, nq), in_specs=[qspec, kspec, kspec], out_specs=qspec,
        out_shape=jax.ShapeDtypeStruct((b, tq_all, SB_W), F32),
        scratch_shapes=[pltpu.VMEM((SB_HEADS, ATTN_BLOCK, ATTN_BLOCK), F32),
                        pltpu.VMEM((SB_W // LANES, ATTN_BLOCK, LANES), F32)],
        compiler_params=_cparams(("parallel", "arbitrary")), name="sb_attn")(q, k, v)


def _half_sizes(c):
    out, m = [], c // 2
    while m >= 1:
        out.append(m)
        m //= 2
    return out


def _block_ref_rows(b, m):
    c, dk = b.shape
    if 2 * m >= SUBLANES:
        pieces = [jnp.broadcast_to(b[a + m - 1:a + m, :], (2 * m, dk)) for a in range(0, c, 2 * m)]
        return pieces[0] if len(pieces) == 1 else jnp.concatenate(pieces, axis=0)
    row8 = lax.broadcasted_iota(jnp.int32, (SUBLANES, dk), 0)
    tiles = []
    for a in range(0, c, SUBLANES):
        tile = None
        for off in range(0, SUBLANES, 2 * m):
            r = jnp.broadcast_to(b[a + off + m - 1:a + off + m, :], (SUBLANES, dk))
            tile = r if tile is None else jnp.where(row8 >= off, r, tile)
        tiles.append(tile)
    return jnp.concatenate(tiles, axis=0)


def _gla_kernel(q_ref, k_ref, la_ref, v_ref, s0_ref, o_ref, st_ref, *, n_valid):
    c = GLA_CHUNK
    ci = pl.program_id(1)

    @pl.when(ci == 0)
    def _():
        st_ref[...] = s0_ref[...]

    r_iota = lax.broadcasted_iota(jnp.int32, (c, c), 0)
    c_iota = lax.broadcasted_iota(jnp.int32, (c, c), 1)
    tril = (r_iota >= c_iota).astype(BF16)
    level = jnp.where(c_iota < r_iota, 31 - lax.clz(r_iota ^ c_iota), jnp.where(c_iota == r_iota, -1, -2))
    valid = (ci * c + lax.broadcasted_iota(jnp.int32, (c, GLA_DK), 0)) < n_valid

    for h in range(GLA_HEADS):
        ks = slice(h * GLA_DK, (h + 1) * GLA_DK)
        vs = slice(h * GLA_DV, (h + 1) * GLA_DV)
        la = jnp.where(valid, la_ref[0, :, ks], 0.0)
        k = jnp.where(valid, k_ref[0, :, ks], 0.0)
        q = q_ref[0, :, ks]
        v = v_ref[0, :, vs]
        l1 = la.astype(BF16)
        r1 = la - l1.astype(F32)
        l2 = r1.astype(BF16)
        l3 = (r1 - l2.astype(F32)).astype(BF16)
        b = (jnp.dot(tril, l1, preferred_element_type=F32) + jnp.dot(tril, l2, preferred_element_type=F32)
             + jnp.dot(tril, l3, preferred_element_type=F32))
        b_last = b[c - 1:c, :]
        st = st_ref[0, h]
        inter = lax.dot_general((q * jnp.exp(b)).astype(BF16), st.astype(BF16), _NT,
                                preferred_element_type=F32)
        scores = jnp.where(level == -1, lax.dot_general(q.astype(BF16), k.astype(BF16), _NT,
                                                        preferred_element_type=F32), 0.0)
        for m in _half_sizes(c):
            ref_rows = _block_ref_rows(b, m)
            qm = (q * jnp.exp(jnp.minimum(b - ref_rows, 0.0))).astype(BF16)
            km = (k * jnp.exp(jnp.minimum(ref_rows - b, 0.0))).astype(BF16)
            scores = jnp.where(level == m.bit_length() - 1,
                               lax.dot_general(qm, km, _NT, preferred_element_type=F32), scores)
        intra = jnp.dot(scores.astype(BF16), v, preferred_element_type=F32)
        o_ref[0, :, vs] = inter + intra
        kd = (k * jnp.exp(b_last - b)).astype(BF16)
        st_ref[0, h] = st * jnp.exp(b_last) + lax.dot_general(v, kd, _TN, preferred_element_type=F32)


def _gla_call(q, k, la, v, s0t, n_valid):
    b, t, _ = q.shape
    c = GLA_CHUNK
    kspec = pl.BlockSpec((1, c, GLA_KW), lambda bi, ci: (bi, ci, 0))
    vspec = pl.BlockSpec((1, c, GLA_VW), lambda bi, ci: (bi, ci, 0))
    sspec = pl.BlockSpec((1, GLA_HEADS, GLA_DV, GLA_DK), lambda bi, ci: (bi, 0, 0, 0))
    return pl.pallas_call(
        functools.partial(_gla_kernel, n_valid=n_valid),
        grid=(b, t // c), in_specs=[kspec, kspec, kspec, vspec, sspec],
        out_specs=[vspec, sspec],
        out_shape=[jax.ShapeDtypeStruct((b, t, GLA_VW), F32),
                   jax.ShapeDtypeStruct((b, GLA_HEADS, GLA_DV, GLA_DK), F32)],
        compiler_params=_cparams(("parallel", "arbitrary")), name="gla")(q, k, la, v, s0t)


def _merge_kernel(h_ref, sb_ref, go_ref, rb_ref, ga_ref, gb_ref, gn_ref, wpa_ref, wpb_ref, wo_ref,
                  g1_ref, b1_ref, out_ref):
    go = go_ref[...]
    parts = []
    for h in range(GLA_HEADS):
        x = go[:, h * GLA_DV:(h + 1) * GLA_DV]
        ms = jnp.mean(x * x, axis=-1, keepdims=True)
        parts.append(x * lax.rsqrt(ms + RMS_EPS) * gn_ref[...])
    o = jnp.concatenate(parts, axis=-1)
    rb = rb_ref[...]
    gated = (o * (rb * _sigmoid(rb))).astype(BF16)
    gla_y = jnp.dot(gated, wpb_ref[...], preferred_element_type=F32)
    sb_y = jnp.dot(sb_ref[...].astype(BF16), wpa_ref[...], preferred_element_type=F32)
    merged = _sigmoid(ga_ref[...]) * sb_y + _sigmoid(gb_ref[...]) * gla_y
    y = jnp.dot(merged.astype(BF16), wo_ref[...], preferred_element_type=F32)
    out_ref[...] = _layer_norm(DN_ALPHA * h_ref[...] + y, g1_ref[...], b1_ref[...])


def _merge_call(h, sb, go, rb, ga, gb, gn, wpa, wpb, wo, g1, b1):
    n, d = h.shape
    tm = min(_row_tile(n), 256)
    row = lambda w: pl.BlockSpec((tm, w), lambda i: (i, 0))
    full = lambda a: pl.BlockSpec(a.shape, lambda i: (0,) * a.ndim)
    args = (h, sb, go, rb, ga, gb, gn, wpa, wpb, wo, g1, b1)
    return pl.pallas_call(
        _merge_kernel, grid=(n // tm,),
        in_specs=[row(d), row(SB_W), row(GLA_VW), row(GLA_VW), row(d), row(d)] + [full(a) for a in args[6:]],
        out_specs=row(d), out_shape=jax.ShapeDtypeStruct((n, d), F32),
        compiler_params=_cparams(("parallel",)), name="merge")(*args)


def _sort_network(n):
    def merge(lo, hi, r):
        step = r * 2
        if step < hi - lo:
            yield from merge(lo, hi, step)
            yield from merge(lo + r, hi, step)
            yield from [(i, i + r) for i in range(lo + r, hi - r, step)]
        else:
            yield (lo, lo + r)

    def sort(lo, hi):
        if hi - lo >= 1:
            mid = lo + (hi - lo) // 2
            yield from sort(lo, mid)
            yield from sort(mid + 1, hi)
            yield from merge(lo, hi, 1)

    return tuple(sort(0, n - 1))


def _bitonic_merge_network(n):
    out, k = [], n // 2
    while k >= 1:
        out.extend((i, i + k) for i in range(n) if not i & k)
        k //= 2
    return tuple(out)


_SORT_TOPK = _sort_network(PEER_TOPK)
_MERGE_TOPK = _bitonic_merge_network(PEER_TOPK)


def _exchange(v, net):
    v = list(v)
    for i, j in net:
        hi, lo = jnp.maximum(v[i], v[j]), jnp.minimum(v[i], v[j])
        v[i], v[j] = hi, lo
    return v


def _merge_top(x, y):
    n = PEER_TOPK
    return _exchange([jnp.maximum(x[k], y[n - 1 - k]) for k in range(n)], _MERGE_TOPK)


def _column_top(s):
    v = _exchange([s[SUBLANES * k:SUBLANES * (k + 1), :] for k in range(PEER_TOPK)], _SORT_TOPK)
    for shift in (4, 2, 1):
        v = _merge_top(v, [pltpu.roll(x, shift, 0) for x in v])
    return v


_CAND_COLS = tuple(PEER_TOPK // (a + 1) for a in range(PEER_TOPK // 2))


def _ranked_top(s):
    rows = s.shape[0]
    iota = lax.broadcasted_iota(jnp.int32, s.shape, 0)
    rank = jnp.full(s.shape, float(PEER_TOPK), F32)
    vals = []
    for k in range(PEER_TOPK):
        m = jnp.max(s, axis=0, keepdims=True)
        first = jnp.min(jnp.where(s == m, iota, rows), axis=0, keepdims=True)
        hit = iota == first
        rank = jnp.where(hit, float(k), rank)
        s = jnp.where(hit, -jnp.inf, s)
        vals.append(m)
    return rank, vals


def _peer_kernel(h_ref, wqt_ref, keys_ref, u_ref, vt_ref, g2_ref, b2_ref, out_ref,
                 qt_scr, s_scr, e1_scr, e2_scr, top_scr, th_scr, stat_scr, flag_scr, acc_scr, hbt_scr, *, nc):
    ci = pl.program_id(1)
    tm = h_ref.shape[0]
    nkeys = keys_ref.shape[1]

    @pl.when(ci == 0)
    def _():
        hbt_scr[...] = h_ref[...].T.astype(BF16)
        qt_scr[...] = jnp.dot(wqt_ref[...], hbt_scr[...], preferred_element_type=F32)
        top_scr[...] = jnp.zeros_like(top_scr)
        flag_scr[...] = jnp.zeros_like(flag_scr)
        sub = lax.broadcasted_iota(jnp.int32, (SUBLANES, tm), 0)

        def per_half(hs, carry):
            hd = lax.shift_right_logical(hs, 1)
            side = hs & 1
            qh = qt_scr[pl.ds(pl.multiple_of(hs * PEER_HALF, PEER_HALF), PEER_HALF), :]
            s = jnp.dot(keys_ref[side], qh, preferred_element_type=F32)
            s_scr[hs] = s
            top = _column_top(s)
            at_least = jnp.sum(jnp.where(s >= top[PEER_TOPK - 1][0:1, :], 1.0, 0.0), axis=0, keepdims=True)
            tie = jnp.where(at_least != float(PEER_TOPK), 1.0, flag_scr[...])
            for k in range(PEER_TOPK):
                top_scr[side, k] = jnp.where(sub == hd, top[k], top_scr[side, k])
                if k:
                    tie = jnp.where(top[k - 1] == top[k], 1.0, tie)
            flag_scr[...] = tie
            return carry

        lax.fori_loop(0, 2 * PEER_HEADS, per_half, 0)

        t1 = [top_scr[0, k] for k in range(PEER_TOPK)]
        t2 = [top_scr[1, k] for k in range(PEER_TOPK)]
        neg = jnp.full((SUBLANES, tm), -jnp.inf, F32)
        lists = [[t1[a] + t2[b] for b in range(nb)] + [neg] * (PEER_TOPK - nb) for a, nb in enumerate(_CAND_COLS)]
        half = PEER_TOPK // 2
        lists.append([t1[a] + t2[0] for a in range(half, PEER_TOPK)] + [neg] * half)
        while len(lists) > 1:
            lists = [_merge_top(lists[i], lists[i + 1]) if i + 1 < len(lists) else lists[i]
                     for i in range(0, len(lists), 2)]
        cv = lists[0]
        tau = cv[PEER_TOPK - 1]
        zsum = jnp.exp(cv[0] - cv[0])
        for k in range(1, PEER_TOPK):
            zsum = zsum + jnp.exp(cv[k] - cv[0])
        stat_scr[1] = zsum
        stat_scr[2] = t1[0]
        stat_scr[3] = t2[0]
        n_reach = jnp.zeros((SUBLANES, tm), F32)
        for a in range(PEER_TOPK):
            th = jnp.full((SUBLANES, tm), jnp.inf, F32)
            for b in range(PEER_TOPK // (a + 1)):
                reach = t1[a] + t2[b] >= tau
                th = jnp.where(reach, t2[b], th)
                n_reach = n_reach + jnp.where(reach, 1.0, 0.0)
            th_scr[a] = th
        any_tie = jnp.max(jnp.where(n_reach != float(PEER_TOPK), 1.0, flag_scr[...])) > 0.0

        @pl.when(jnp.logical_not(any_tie))
        def _():
            def per_head(hd, carry):
                zr = stat_scr[1, pl.ds(hd, 1), :]
                m1 = stat_scr[2, pl.ds(hd, 1), :]
                m2 = stat_scr[3, pl.ds(hd, 1), :]
                s1 = s_scr[2 * hd]
                e1_scr[hd] = jnp.exp(s1 - m1) / zr
                e2_scr[hd] = jnp.exp(s_scr[2 * hd + 1] - m2)
                thr = jnp.full((nkeys, tm), jnp.inf, F32)
                for a in range(PEER_TOPK):
                    thr = jnp.where(s1 == top_scr[0, a, pl.ds(hd, 1), :], th_scr[a, pl.ds(hd, 1), :], thr)
                s_scr[2 * hd] = thr
                return carry

            lax.fori_loop(0, PEER_HEADS, per_head, 0)

        @pl.when(any_tie)
        def _():
            def per_head(hd, carry):
                s1 = s_scr[2 * hd]
                s2 = s_scr[2 * hd + 1]
                rank1, v1 = _ranked_top(s1)
                rank2, v2 = _ranked_top(s2)
                v2col = jnp.concatenate(v2, axis=0)
                cand = jnp.concatenate([v1[a] + v2col for a in range(PEER_TOPK)], axis=0)
                crank, cvals = _ranked_top(cand)
                taken = jnp.where(crank < float(PEER_TOPK), 1.0, 0.0)
                pairs = jnp.zeros((nkeys, tm), F32)
                for a in range(PEER_TOPK):
                    n_a = jnp.sum(taken[a * PEER_TOPK:(a + 1) * PEER_TOPK, :], axis=0, keepdims=True)
                    pairs = jnp.where(rank1 == float(a), n_a, pairs)
                zs = jnp.exp(cvals[0] - cvals[0])
                for k in range(1, PEER_TOPK):
                    zs = zs + jnp.exp(cvals[k] - cvals[0])
                e1_scr[hd] = jnp.exp(s1 - v1[0]) / zs
                e2_scr[hd] = jnp.exp(s2 - v2[0])
                s_scr[2 * hd] = jnp.where(pairs > 0.0, 1.0 - pairs, jnp.inf)
                s_scr[2 * hd + 1] = jnp.where(rank2 < float(PEER_TOPK), -rank2, -jnp.inf)
                return carry

            lax.fori_loop(0, PEER_HEADS, per_head, 0)

        acc_scr[...] = jnp.zeros_like(acc_scr)

    piece = 2 * nkeys
    npiece = PEER_GROUP // 2
    sub = min(tm, PEER_SUBTILE)

    def gate_rows(i1, tok):
        w = jnp.zeros((nkeys, sub), F32)
        for hd in range(PEER_HEADS):
            thr = s_scr[2 * hd, pl.ds(i1, 1), tok]
            e1r = e1_scr[hd, pl.ds(i1, 1), tok]
            w = w + jnp.where(s_scr[2 * hd + 1, :, tok] >= thr, e2_scr[hd, :, tok] * e1r, 0.0)
        return w

    for lo in range(0, tm, sub):
        tok = slice(lo, lo + sub)
        for pi in range(npiece):
            rows = slice(pi * piece, (pi + 1) * piece)
            pre = jnp.dot(u_ref[rows, :], hbt_scr[:, tok], preferred_element_type=F32)
            i1 = ci * PEER_GROUP + 2 * pi
            w = jnp.concatenate([gate_rows(i1, tok), gate_rows(i1 + 1, tok)], axis=0)
            act = 0.5 * pre * (1.0 + lax.erf(pre * (2.0 ** -0.5)))
            a = (act * w).astype(BF16)
            acc_scr[:, tok] += jnp.dot(vt_ref[:, rows], a, preferred_element_type=F32)

    @pl.when(ci == nc - 1)
    def _():
        y = acc_scr[...].T
        out_ref[...] = _layer_norm(DN_ALPHA * h_ref[...] + y, g2_ref[...], b2_ref[...])


def _peer_call(h, wqt, keys, u, vt, g2, b2):
    n, d = h.shape
    nkeys = keys.shape[1]
    tm = _row_tile(n)
    g = PEER_GROUP
    nc = nkeys // g
    hspec = pl.BlockSpec((tm, d), lambda i, c: (i, 0))
    full = lambda a: pl.BlockSpec(a.shape, lambda i, c: (0,) * a.ndim)
    assert nkeys == PEER_TOPK * SUBLANES and PEER_HEADS == SUBLANES
    dense = pltpu.VMEM((PEER_HEADS, nkeys, tm), F32)
    return pl.pallas_call(
        functools.partial(_peer_kernel, nc=nc), grid=(n // tm, nc),
        in_specs=[hspec, full(wqt), full(keys),
                  pl.BlockSpec((g * nkeys, d), lambda i, c: (c, 0)),
                  pl.BlockSpec((d, g * nkeys), lambda i, c: (0, c)), full(g2), full(b2)],
        out_specs=hspec, out_shape=jax.ShapeDtypeStruct((n, d), F32),
        scratch_shapes=[pltpu.VMEM((PEER_HEADS * 2 * PEER_HALF, tm), F32),
                        pltpu.VMEM((2 * PEER_HEADS, nkeys, tm), F32), dense, dense,
                        pltpu.VMEM((2, PEER_TOPK, SUBLANES, tm), F32), pltpu.VMEM((PEER_TOPK, SUBLANES, tm), F32),
                        pltpu.VMEM((4, SUBLANES, tm), F32), pltpu.VMEM((SUBLANES, tm), F32),
                        pltpu.VMEM((d, tm), F32), pltpu.VMEM((d, tm), BF16)],
        compiler_params=_cparams(("parallel", "arbitrary")), name="peer")(h, wqt, keys, u, vt, g2, b2)


def _round_up(x, m):
    return -(-x // m) * m


def _pad_rows(x, rows):
    return jnp.pad(x, ((0, 0), (0, rows - x.shape[1]), (0, 0)))


def _pad_flat(x, rows):
    return jnp.pad(x, ((0, rows - x.shape[0]), (0, 0)))


def kernel(x_prompt, x_sample, cache_k, cache_v, state_gla, meta, ln_in_g, ln_in_b, w_in, w_gk2, b_gk,
           gla_norm_g, w_pa, w_pb, w_o, ln1_g, ln1_b, peer_wq, peer_subkeys, peer_u, peer_v, ln2_g, ln2_b):
    bp, s_len, d = x_prompt.shape
    bd, t_new, _ = x_sample.shape
    past = cache_k.shape[2]
    n_len = N_META + s_len
    keep = min(s_len, past)
    lp = _round_up(n_len, ATTN_BLOCK)
    ts = _round_up(t_new, ATTN_BLOCK)
    tg = _round_up(t_new, GLA_CHUNK)
    tk_s = _round_up(N_META + past + t_new, ATTN_BLOCK)
    row = lambda a: a.reshape(1, -1)

    meta_rows = jnp.broadcast_to(meta[None].astype(x_prompt.dtype), (bp, N_META, d))
    xp = _pad_rows(jnp.concatenate([meta_rows, x_prompt], axis=1), lp).reshape(bp * lp, d)
    ns = bd * t_new
    ns_p = _round_up(ns, LANES)
    xs = _pad_flat(x_sample.reshape(ns, d), ns_p)
    hp = _ln_call(xp, ln_in_g, ln_in_b)
    hs = _ln_call(xs, ln_in_g, ln_in_b)

    outs = {k: [] for k in ("nk_p", "nv_p", "st_p", "nk_s", "nv_s", "st_s")}
    for l in range(DEPTH):
        glr0 = _SEC[7]
        w_main = jnp.concatenate([w_in[l][:, :glr0], w_in[l][:, glr0 + GLA_RANK:]], axis=1).astype(BF16)
        w_glr = jnp.pad(w_in[l][:, glr0:glr0 + GLA_RANK], ((0, 0), (0, LANES - GLA_RANK))).astype(BF16)
        w_gk2p = jnp.pad(w_gk2[l], ((0, LANES - GLA_RANK), (0, 0)))
        merge_w = (row(gla_norm_g[l]), w_pa[l].astype(BF16), w_pb[l].astype(BF16), w_o[l].astype(BF16),
                   row(ln1_g[l]), row(ln1_b[l]))
        peer_w = (peer_wq[l].T.astype(BF16), peer_subkeys[l], peer_u[l].astype(BF16),
                  peer_v[l].T.astype(BF16), row(ln2_g[l]), row(ln2_b[l]))

        qa, kab, vab, ka, va, qb, kb, vb, la, rb, ga, gb = _inproj_call(hp, w_main, w_glr, w_gk2p, row(b_gk[l]))
        b3 = lambda a: a.reshape(bp, lp, a.shape[-1])
        sb = _sb_attn_call(b3(qa), b3(kab), b3(vab), 0)
        s0t = jnp.zeros((bp, GLA_HEADS, GLA_DV, GLA_DK), F32)
        go, stp = _gla_call(b3(qb), b3(kb), b3(la), b3(vb), s0t, n_len)
        hp = _merge_call(hp, sb.reshape(bp * lp, SB_W), go.reshape(bp * lp, GLA_VW), rb, ga, gb, *merge_w)
        hp = _peer_call(hp, *peer_w)
        ka3 = ka.reshape(bp, lp, SB_HEADS, SB_DIM)
        va3 = va.reshape(bp, lp, SB_HEADS, SB_DIM)
        outs["nk_p"].append(ka3[:, n_len - keep:n_len])
        outs["nv_p"].append(va3[:, n_len - keep:n_len])
        outs["st_p"].append(jnp.swapaxes(stp, 2, 3).astype(state_gla.dtype))

        qa2, _, _, ka2, va2, qb2, kb2, vb2, la2, rb2, ga2, gb2 = _inproj_call(
            hs, w_main, w_glr, w_gk2p, row(b_gk[l]))
        s3 = lambda a: a[:ns].reshape(bd, t_new, a.shape[-1])
        flat = lambda a: _pad_flat(a[:, :t_new].reshape(ns, a.shape[-1]), ns_p)
        ka2n = s3(ka2)
        va2n = s3(va2)
        meta_k = jnp.broadcast_to(ka3[:1, :N_META].reshape(1, N_META, SB_W), (bd, N_META, SB_W))
        meta_v = jnp.broadcast_to(va3[:1, :N_META].reshape(1, N_META, SB_W), (bd, N_META, SB_W))
        k_all = jnp.concatenate([meta_k, cache_k[l].reshape(bd, past, SB_W), ka2n], axis=1)
        v_all = jnp.concatenate([meta_v, cache_v[l].reshape(bd, past, SB_W), va2n], axis=1)
        sb2 = _sb_attn_call(_pad_rows(s3(qa2), ts), _pad_rows(k_all, tk_s).astype(BF16),
                            _pad_rows(v_all, tk_s).astype(BF16), N_META + past)
        g3 = lambda a: _pad_rows(s3(a), tg)
        go2, sts = _gla_call(g3(qb2), g3(kb2), g3(la2), g3(vb2), jnp.swapaxes(state_gla[l], 2, 3), t_new)
        hs = _merge_call(hs, flat(sb2), flat(go2), rb2, ga2, gb2, *merge_w)
        hs = _peer_call(hs, *peer_w)
        outs["nk_s"].append(ka2n.reshape(bd, t_new, SB_HEADS, SB_DIM))
        outs["nv_s"].append(va2n.reshape(bd, t_new, SB_HEADS, SB_DIM))
        outs["st_s"].append(jnp.swapaxes(sts, 2, 3).astype(state_gla.dtype))

    y_prompt = hp.reshape(bp, lp, d)[:, N_META:n_len]
    y_sample = hs[:ns].reshape(bd, t_new, d)
    return (y_prompt, y_sample, jnp.stack(outs["nk_p"]), jnp.stack(outs["nv_p"]), jnp.stack(outs["st_p"]),
            jnp.stack(outs["nk_s"]), jnp.stack(outs["nv_s"]), jnp.stack(outs["st_s"]))
```

```python
import functools

import jax
import jax.numpy as jnp
from jax import lax
from jax.experimental import pallas as pl
from jax.experimental.pallas import tpu as pltpu

F32 = jnp.float32
BF16 = jnp.bfloat16

D_MODEL = 1024
DEPTH = 2
N_META = 16
SB_HEADS = 8
SB_DIM = 64
SB_W = SB_HEADS * SB_DIM
GLA_HEADS = 4
GLA_DK = 128
GLA_DV = 256
GLA_KW = GLA_HEADS * GLA_DK
GLA_VW = GLA_HEADS * GLA_DV
GLA_RANK = 16
GLA_TAU = 16.0
PEER_HEADS = 8
PEER_HALF = 128
PEER_TOPK = 16
DN_ALPHA = float((2 * DEPTH) ** 0.25)
LN_EPS = 1e-5
RMS_EPS = 1e-6

LANES = 128
SUBLANES = 8
VMEM_LIMIT_BYTES = 56 * 1024 * 1024

ATTN_BLOCK = 128
GLA_CHUNK = 128
PEER_GROUP = 32
EXP_ZERO_BELOW = -104.0

_NT = (((1,), (1,)), ((), ()))
_TN = (((0,), (0,)), ((), ()))


def _cparams(sem, flags=None):
    return pltpu.CompilerParams(dimension_semantics=sem, vmem_limit_bytes=VMEM_LIMIT_BYTES, flags=flags)


def _row_tile(n):
    for t in (512, 256, 128):
        if n % t == 0:
            return t
    raise ValueError(f"token count {n} is not a multiple of 128")


def _layer_norm(x, g, b):
    mu = jnp.mean(x, axis=-1, keepdims=True)
    xc = x - mu
    var = jnp.mean(xc * xc, axis=-1, keepdims=True)
    return xc * lax.rsqrt(var + LN_EPS) * g + b


def _softplus(x):
    return jnp.maximum(x, 0.0) + jnp.log(1.0 + jnp.exp(-jnp.abs(x)))


def _sigmoid(x):
    return 1.0 / (1.0 + jnp.exp(-x))


def _ln_kernel(x_ref, g_ref, b_ref, o_ref):
    o_ref[...] = _layer_norm(x_ref[...], g_ref[...], b_ref[...])


def _ln_call(x, g, b):
    n, d = x.shape
    tm = _row_tile(n)
    row = pl.BlockSpec((tm, d), lambda i: (i, 0))
    vec = pl.BlockSpec((1, d), lambda i: (0, 0))
    return pl.pallas_call(
        _ln_kernel, grid=(n // tm,), in_specs=[row, vec, vec], out_specs=row,
        out_shape=jax.ShapeDtypeStruct((n, d), F32), compiler_params=_cparams(("parallel",)),
        name="ln_in")(x, g.reshape(1, d), b.reshape(1, d))


_SEC = (0, 512, 1024, 1536, 2048, 2560, 3584, 4608, 5632, 6656)


def _inproj_kernel(h_ref, w_ref, wg_ref, wgk2_ref, bgk_ref,
                   qa_ref, kab_ref, vab_ref, ka_ref, va_ref, qb_ref, kb_ref, vb_ref, la_ref,
                   rb_ref, ga_ref, gb_ref):
    hb = h_ref[...].astype(BF16)

    def sec(i):
        return jnp.dot(hb, w_ref[:, _SEC[i]:_SEC[i + 1]], preferred_element_type=F32)

    qa_ref[...] = (sec(0) * (SB_DIM ** -0.5)).astype(BF16)
    ka = sec(1)
    ka_ref[...] = ka
    kab_ref[...] = ka.astype(BF16)
    va = sec(2)
    va_ref[...] = va
    vab_ref[...] = va.astype(BF16)
    qb_ref[...] = sec(3) * (GLA_DK ** -0.5)
    kb_ref[...] = sec(4)
    vb_ref[...] = sec(5).astype(BF16)
    rb = sec(6)
    rb_ref[...] = (rb * _sigmoid(rb)).astype(BF16)
    ga_ref[...] = _sigmoid(sec(7)).astype(BF16)
    gb_ref[...] = _sigmoid(sec(8)).astype(BF16)
    glr = jnp.dot(hb, wg_ref[...], preferred_element_type=F32)
    x = jnp.dot(glr, wgk2_ref[...], preferred_element_type=F32) + bgk_ref[...]
    la_ref[...] = -_softplus(-x) * (1.0 / GLA_TAU)


def _inproj_call(h, w_main, w_glr, w_gk2p, b_gk):
    n, d = h.shape
    tm = min(_row_tile(n), 256)
    row = lambda w: pl.BlockSpec((tm, w), lambda i: (i, 0))
    full = lambda a: pl.BlockSpec(a.shape, lambda i: (0,) * a.ndim)
    outs = [(SB_W, BF16), (SB_W, BF16), (SB_W, BF16), (SB_W, F32), (SB_W, F32),
            (GLA_KW, F32), (GLA_KW, F32), (GLA_VW, BF16), (GLA_KW, F32),
            (GLA_VW, BF16), (D_MODEL, BF16), (D_MODEL, BF16)]
    return pl.pallas_call(
        _inproj_kernel, grid=(n // tm,),
        in_specs=[row(d), full(w_main), full(w_glr), full(w_gk2p), full(b_gk)],
        out_specs=[row(w) for w, _ in outs],
        out_shape=[jax.ShapeDtypeStruct((n, w), t) for w, t in outs],
        compiler_params=_cparams(("parallel",)), name="in_proj")(h, w_main, w_glr, w_gk2p, b_gk)


def _sb_attn_kernel(q_ref, k_ref, v_ref, o_ref, carry_scr, acc_scr, *, q_off, nk):
    tq = tk = ATTN_BLOCK
    npair = SB_W // LANES
    i = pl.program_id(1)
    lane = lax.broadcasted_iota(jnp.int32, (tq, LANES), 1)
    r_iota = lax.broadcasted_iota(jnp.int32, (tq, tk), 0)
    c_iota = lax.broadcasted_iota(jnp.int32, (tq, tk), 1)
    later_ones = jnp.concatenate([(r_iota > c_iota).astype(BF16), jnp.ones((tk, tk), BF16)], axis=1)
    qpos = q_off + i * tq + r_iota
    j_start = jnp.minimum((q_off + i * tq + tq - 2) // tk, nk - 1)
    carry_scr[...] = jnp.zeros_like(carry_scr)
    acc_scr[...] = jnp.zeros_like(acc_scr)

    def cond(c):
        j, cmax = c
        return jnp.logical_and(j >= 0, cmax > EXP_ZERO_BELOW)

    def body(c):
        j, _ = c
        off = pl.multiple_of(j * tk, tk)
        mask = (off + c_iota) < qpos
        heads = range(SB_HEADS)
        zs = []
        for h in heads:
            cols = slice((h // 2) * LANES, (h // 2 + 1) * LANES)
            q = q_ref[0, :, cols]
            qm = jnp.where(lane < SB_DIM if h % 2 == 0 else lane >= SB_DIM, q, jnp.zeros_like(q))
            zs.append(lax.dot_general(qm, k_ref[0, pl.ds(off, tk), cols], _NT, preferred_element_type=F32))
        lbs, css = [], []
        for h in heads:
            sp = _softplus(zs[h])
            lbs.append(zs[h] - sp)
            lk = jnp.where(mask, -sp, 0.0)
            hi = lk.astype(BF16)
            lo = (lk - hi.astype(F32)).astype(BF16)
            css.append(jnp.dot(hi, later_ones, preferred_element_type=F32)
                       + jnp.dot(lo, later_ones, preferred_element_type=F32))
        cmax = None
        pvs = []
        for h in heads:
            cols = slice((h // 2) * LANES, (h // 2 + 1) * LANES)
            carry = carry_scr[h]
            w = jnp.where(mask, jnp.exp(lbs[h] + css[h][:, :tk] + carry), 0.0).astype(BF16)
            pvs.append(jnp.dot(w, v_ref[0, pl.ds(off, tk), cols], preferred_element_type=F32))
            carry = carry + css[h][:, tk:]
            carry_scr[h] = carry
            cmax = carry if cmax is None else jnp.maximum(cmax, carry)
        for p in range(npair):
            acc_scr[p] += jnp.where(lane < SB_DIM, pvs[2 * p], pvs[2 * p + 1])
        return j - 1, jnp.max(cmax)

    lax.while_loop(cond, body, (j_start, jnp.float32(0.0)))
    for p in range(npair):
        o_ref[0, :, p * LANES:(p + 1) * LANES] = acc_scr[p].astype(o_ref.dtype)


def _sb_attn_call(q, k, v, q_off):
    b, tq_all, _ = q.shape
    tk_all = k.shape[1]
    nq, nk = tq_all // ATTN_BLOCK, tk_all // ATTN_BLOCK
    qspec = pl.BlockSpec((1, ATTN_BLOCK, SB_W), lambda bi, i: (bi, i, 0))
    kspec = pl.BlockSpec((1, tk_all, SB_W), lambda bi, i: (bi, 0, 0))
    return pl.pallas_call(
        functools.partial(_sb_attn_kernel, q_off=q_off, nk=nk),
        grid=(b, nq), in_specs=[qspec, kspec, kspec], out_specs=qspec,
        out_shape=jax.ShapeDtypeStruct((b, tq_all, SB_W), BF16),
        scratch_shapes=[pltpu.VMEM((SB_HEADS, ATTN_BLOCK, ATTN_BLOCK), F32),
                        pltpu.VMEM((SB_W // LANES, ATTN_BLOCK, LANES), F32)],
        compiler_params=_cparams(("parallel", "arbitrary")), name="sb_attn")(q, k, v)


def _half_sizes(c):
    out, m = [], c // 2
    while m >= 1:
        out.append(m)
        m //= 2
    return out


def _block_ref_rows(b, m):
    c, dk = b.shape
    if 2 * m >= SUBLANES:
        pieces = [jnp.broadcast_to(b[a + m - 1:a + m, :], (2 * m, dk)) for a in range(0, c, 2 * m)]
        return pieces[0] if len(pieces) == 1 else jnp.concatenate(pieces, axis=0)
    row8 = lax.broadcasted_iota(jnp.int32, (SUBLANES, dk), 0)
    tiles = []
    for a in range(0, c, SUBLANES):
        tile = None
        for off in range(0, SUBLANES, 2 * m):
            r = jnp.broadcast_to(b[a + off + m - 1:a + off + m, :], (SUBLANES, dk))
            tile = r if tile is None else jnp.where(row8 >= off, r, tile)
        tiles.append(tile)
    return jnp.concatenate(tiles, axis=0)


def _gla_kernel(q_ref, k_ref, la_ref, v_ref, s0_ref, o_ref, st_ref, *, n_valid):
    c = GLA_CHUNK
    ci = pl.program_id(1)

    @pl.when(ci == 0)
    def _():
        st_ref[...] = s0_ref[...]

    r_iota = lax.broadcasted_iota(jnp.int32, (c, c), 0)
    c_iota = lax.broadcasted_iota(jnp.int32, (c, c), 1)
    tril = (r_iota >= c_iota).astype(BF16)
    level = jnp.where(c_iota < r_iota, 31 - lax.clz(r_iota ^ c_iota), jnp.where(c_iota == r_iota, -1, -2))
    valid = (ci * c + lax.broadcasted_iota(jnp.int32, (c, GLA_DK), 0)) < n_valid

    for h in range(GLA_HEADS):
        ks = slice(h * GLA_DK, (h + 1) * GLA_DK)
        vs = slice(h * GLA_DV, (h + 1) * GLA_DV)
        la = jnp.where(valid, la_ref[0, :, ks], 0.0)
        k = jnp.where(valid, k_ref[0, :, ks], 0.0)
        q = q_ref[0, :, ks]
        v = v_ref[0, :, vs]
        l1 = la.astype(BF16)
        r1 = la - l1.astype(F32)
        l2 = r1.astype(BF16)
        l3 = (r1 - l2.astype(F32)).astype(BF16)
        b = (jnp.dot(tril, l1, preferred_element_type=F32) + jnp.dot(tril, l2, preferred_element_type=F32)
             + jnp.dot(tril, l3, preferred_element_type=F32))
        b_last = b[c - 1:c, :]
        st = st_ref[0, h]
        inter = lax.dot_general((q * jnp.exp(b)).astype(BF16), st.astype(BF16), _NT,
                                preferred_element_type=F32)
        scores = jnp.where(level == -1, lax.dot_general(q.astype(BF16), k.astype(BF16), _NT,
                                                        preferred_element_type=F32), 0.0)
        for m in _half_sizes(c):
            ref_rows = _block_ref_rows(b, m)
            qm = (q * jnp.exp(jnp.minimum(b - ref_rows, 0.0))).astype(BF16)
            km = (k * jnp.exp(jnp.minimum(ref_rows - b, 0.0))).astype(BF16)
            scores = jnp.where(level == m.bit_length() - 1,
                               lax.dot_general(qm, km, _NT, preferred_element_type=F32), scores)
        intra = jnp.dot(scores.astype(BF16), v, preferred_element_type=F32)
        o_ref[0, :, vs] = inter + intra
        kd = (k * jnp.exp(b_last - b)).astype(BF16)
        st_ref[0, h] = st * jnp.exp(b_last) + lax.dot_general(v, kd, _TN, preferred_element_type=F32)


def _gla_call(q, k, la, v, s0t, n_valid):
    b, t, _ = q.shape
    c = GLA_CHUNK
    kspec = pl.BlockSpec((1, c, GLA_KW), lambda bi, ci: (bi, ci, 0))
    vspec = pl.BlockSpec((1, c, GLA_VW), lambda bi, ci: (bi, ci, 0))
    sspec = pl.BlockSpec((1, GLA_HEADS, GLA_DV, GLA_DK), lambda bi, ci: (bi, 0, 0, 0))
    return pl.pallas_call(
        functools.partial(_gla_kernel, n_valid=n_valid),
        grid=(b, t // c), in_specs=[kspec, kspec, kspec, vspec, sspec],
        out_specs=[vspec, sspec],
        out_shape=[jax.ShapeDtypeStruct((b, t, GLA_VW), F32),
                   jax.ShapeDtypeStruct((b, GLA_HEADS, GLA_DV, GLA_DK), F32)],
        compiler_params=_cparams(("parallel", "arbitrary")), name="gla")(q, k, la, v, s0t)


def _merge_kernel(h_ref, sb_ref, go_ref, rb_ref, ga_ref, gb_ref, gn_ref, wpa_ref, wpb_ref, wo_ref,
                  g1_ref, b1_ref, out_ref):
    go = go_ref[...]
    parts = []
    for h in range(GLA_HEADS):
        x = go[:, h * GLA_DV:(h + 1) * GLA_DV]
        ms = jnp.mean(x * x, axis=-1, keepdims=True)
        parts.append(x * lax.rsqrt(ms + RMS_EPS) * gn_ref[...])
    o = jnp.concatenate(parts, axis=-1)
    gated = (o * rb_ref[...]).astype(BF16)
    gla_y = jnp.dot(gated, wpb_ref[...], preferred_element_type=F32)
    sb_y = jnp.dot(sb_ref[...], wpa_ref[...], preferred_element_type=F32)
    merged = ga_ref[...] * sb_y + gb_ref[...] * gla_y
    y = jnp.dot(merged.astype(BF16), wo_ref[...], preferred_element_type=F32)
    out_ref[...] = _layer_norm(DN_ALPHA * h_ref[...] + y, g1_ref[...], b1_ref[...])


def _merge_call(h, sb, go, rb, ga, gb, gn, wpa, wpb, wo, g1, b1):
    n, d = h.shape
    tm = min(_row_tile(n), 256)
    row = lambda w: pl.BlockSpec((tm, w), lambda i: (i, 0))
    full = lambda a: pl.BlockSpec(a.shape, lambda i: (0,) * a.ndim)
    args = (h, sb, go, rb, ga, gb, gn, wpa, wpb, wo, g1, b1)
    return pl.pallas_call(
        _merge_kernel, grid=(n // tm,),
        in_specs=[row(d), row(SB_W), row(GLA_VW), row(GLA_VW), row(d), row(d)] + [full(a) for a in args[6:]],
        out_specs=row(d), out_shape=jax.ShapeDtypeStruct((n, d), F32),
        compiler_params=_cparams(("parallel",)), name="merge")(*args)


def _sort_network(n):
    def merge(lo, hi, r):
        step = r * 2
        if step < hi - lo:
            yield from merge(lo, hi, step)
            yield from merge(lo + r, hi, step)
            yield from [(i, i + r) for i in range(lo + r, hi - r, step)]
        else:
            yield (lo, lo + r)

    def sort(lo, hi):
        if hi - lo >= 1:
            mid = lo + (hi - lo) // 2
            yield from sort(lo, mid)
            yield from sort(mid + 1, hi)
            yield from merge(lo, hi, 1)

    return tuple(sort(0, n - 1))


def _bitonic_merge_network(n):
    out, k = [], n // 2
    while k >= 1:
        out.extend((i, i + k) for i in range(n) if not i & k)
        k //= 2
    return tuple(out)


_SORT_TOPK = _sort_network(PEER_TOPK)
_MERGE_TOPK = _bitonic_merge_network(PEER_TOPK)


def _exchange(v, net):
    v = list(v)
    for i, j in net:
        hi, lo = jnp.maximum(v[i], v[j]), jnp.minimum(v[i], v[j])
        v[i], v[j] = hi, lo
    return v


def _merge_top(x, y):
    n = PEER_TOPK
    return _exchange([jnp.maximum(x[k], y[n - 1 - k]) for k in range(n)], _MERGE_TOPK)


def _column_top(s):
    v = _exchange([s[SUBLANES * k:SUBLANES * (k + 1), :] for k in range(PEER_TOPK)], _SORT_TOPK)
    for shift in (4, 2, 1):
        v = _merge_top(v, [pltpu.roll(x, shift, 0) for x in v])
    return v


_CAND_COLS = tuple(PEER_TOPK // (a + 1) for a in range(PEER_TOPK // 2))


def _ranked_top(s):
    rows = s.shape[0]
    iota = lax.broadcasted_iota(jnp.int32, s.shape, 0)
    rank = jnp.full(s.shape, float(PEER_TOPK), F32)
    vals = []
    for k in range(PEER_TOPK):
        m = jnp.max(s, axis=0, keepdims=True)
        first = jnp.min(jnp.where(s == m, iota, rows), axis=0, keepdims=True)
        hit = iota == first
        rank = jnp.where(hit, float(k), rank)
        s = jnp.where(hit, -jnp.inf, s)
        vals.append(m)
    return rank, vals


def _peer_kernel(h_ref, wqt_ref, keys_ref, u_ref, vt_ref, g2_ref, b2_ref, out_ref,
                 qt_scr, s_scr, e1_scr, e2_scr, top_scr, th_scr, stat_scr, flag_scr, acc_scr, hbt_scr, *, nc):
    ci = pl.program_id(1)
    tm = h_ref.shape[0]
    nkeys = keys_ref.shape[1]

    @pl.when(ci == 0)
    def _():
        hbt_scr[...] = h_ref[...].T.astype(BF16)
        qt_scr[...] = jnp.dot(wqt_ref[...], hbt_scr[...], preferred_element_type=F32)
        top_scr[...] = jnp.zeros_like(top_scr)
        flag_scr[...] = jnp.zeros_like(flag_scr)
        sub = lax.broadcasted_iota(jnp.int32, (SUBLANES, tm), 0)

        def per_half(hs, carry):
            hd = lax.shift_right_logical(hs, 1)
            side = hs & 1
            qh = qt_scr[pl.ds(pl.multiple_of(hs * PEER_HALF, PEER_HALF), PEER_HALF), :]
            s = jnp.dot(keys_ref[side], qh, preferred_element_type=F32)
            s_scr[hs] = s
            top = _column_top(s)
            at_least = jnp.sum(jnp.where(s >= top[PEER_TOPK - 1][0:1, :], 1.0, 0.0), axis=0, keepdims=True)
            tie = jnp.where(at_least != float(PEER_TOPK), 1.0, flag_scr[...])
            for k in range(PEER_TOPK):
                top_scr[side, k] = jnp.where(sub == hd, top[k], top_scr[side, k])
                if k:
                    tie = jnp.where(top[k - 1] == top[k], 1.0, tie)
            flag_scr[...] = tie
            return carry

        lax.fori_loop(0, 2 * PEER_HEADS, per_half, 0)

        t1 = [top_scr[0, k] for k in range(PEER_TOPK)]
        t2 = [top_scr[1, k] for k in range(PEER_TOPK)]
        neg = jnp.full((SUBLANES, tm), -jnp.inf, F32)
        lists = [[t1[a] + t2[b] for b in range(nb)] + [neg] * (PEER_TOPK - nb) for a, nb in enumerate(_CAND_COLS)]
        half = PEER_TOPK // 2
        lists.append([t1[a] + t2[0] for a in range(half, PEER_TOPK)] + [neg] * half)
        while len(lists) > 1:
            lists = [_merge_top(lists[i], lists[i + 1]) if i + 1 < len(lists) else lists[i]
                     for i in range(0, len(lists), 2)]
        cv = lists[0]
        tau = cv[PEER_TOPK - 1]
        zsum = jnp.exp(cv[0] - cv[0])
        for k in range(1, PEER_TOPK):
            zsum = zsum + jnp.exp(cv[k] - cv[0])
        stat_scr[1] = zsum
        stat_scr[2] = t1[0]
        stat_scr[3] = t2[0]
        n_reach = jnp.zeros((SUBLANES, tm), F32)
        for a in range(PEER_TOPK):
            th = jnp.full((SUBLANES, tm), jnp.inf, F32)
            for b in range(PEER_TOPK // (a + 1)):
                reach = t1[a] + t2[b] >= tau
                th = jnp.where(reach, t2[b], th)
                n_reach = n_reach + jnp.where(reach, 1.0, 0.0)
            th_scr[a] = th
        any_tie = jnp.max(jnp.where(n_reach != float(PEER_TOPK), 1.0, flag_scr[...])) > 0.0

        @pl.when(jnp.logical_not(any_tie))
        def _():
            def per_head(hd, carry):
                zr = stat_scr[1, pl.ds(hd, 1), :]
                m1 = stat_scr[2, pl.ds(hd, 1), :]
                m2 = stat_scr[3, pl.ds(hd, 1), :]
                s1 = s_scr[2 * hd]
                e1_scr[hd] = jnp.exp(s1 - m1) * (0.5 / zr)
                e2_scr[hd] = jnp.exp(s_scr[2 * hd + 1] - m2)
                thr = jnp.full((nkeys, tm), jnp.inf, F32)
                for a in range(PEER_TOPK):
                    thr = jnp.where(s1 == top_scr[0, a, pl.ds(hd, 1), :], th_scr[a, pl.ds(hd, 1), :], thr)
                s_scr[2 * hd] = thr
                return carry

            lax.fori_loop(0, PEER_HEADS, per_head, 0)

        @pl.when(any_tie)
        def _():
            def per_head(hd, carry):
                s1 = s_scr[2 * hd]
                s2 = s_scr[2 * hd + 1]
                rank1, v1 = _ranked_top(s1)
                rank2, v2 = _ranked_top(s2)
                v2col = jnp.concatenate(v2, axis=0)
                cand = jnp.concatenate([v1[a] + v2col for a in range(PEER_TOPK)], axis=0)
                crank, cvals = _ranked_top(cand)
                taken = jnp.where(crank < float(PEER_TOPK), 1.0, 0.0)
                pairs = jnp.zeros((nkeys, tm), F32)
                for a in range(PEER_TOPK):
                    n_a = jnp.sum(taken[a * PEER_TOPK:(a + 1) * PEER_TOPK, :], axis=0, keepdims=True)
                    pairs = jnp.where(rank1 == float(a), n_a, pairs)
                zs = jnp.exp(cvals[0] - cvals[0])
                for k in range(1, PEER_TOPK):
                    zs = zs + jnp.exp(cvals[k] - cvals[0])
                e1_scr[hd] = jnp.exp(s1 - v1[0]) * (0.5 / zs)
                e2_scr[hd] = jnp.exp(s2 - v2[0])
                s_scr[2 * hd] = jnp.where(pairs > 0.0, 1.0 - pairs, jnp.inf)
                s_scr[2 * hd + 1] = jnp.where(rank2 < float(PEER_TOPK), -rank2, -jnp.inf)
                return carry

            lax.fori_loop(0, PEER_HEADS, per_head, 0)

        acc_scr[...] = jnp.zeros_like(acc_scr)

    piece = 2 * nkeys
    npiece = PEER_GROUP // 2

    def gate_rows(i1):
        w = None
        for hd in range(PEER_HEADS):
            thr = s_scr[2 * hd, pl.ds(i1, 1), :]
            e1r = e1_scr[hd, pl.ds(i1, 1), :]
            t = jnp.where(s_scr[2 * hd + 1] >= thr, e2_scr[hd] * e1r, 0.0)
            w = t if w is None else w + t
        return w

    for pi in range(npiece):
        rows = slice(pi * piece, (pi + 1) * piece)
        pre = jnp.dot(u_ref[rows, :], hbt_scr[...], preferred_element_type=F32)
        i1 = ci * PEER_GROUP + 2 * pi
        w = jnp.concatenate([gate_rows(i1), gate_rows(i1 + 1)], axis=0)
        act = pre * (1.0 + lax.erf(pre * (2.0 ** -0.5)))
        a = (act * w).astype(BF16)
        acc_scr[...] += jnp.dot(vt_ref[:, rows], a, preferred_element_type=F32)

    @pl.when(ci == nc - 1)
    def _():
        y = acc_scr[...].T
        out_ref[...] = _layer_norm(DN_ALPHA * h_ref[...] + y, g2_ref[...], b2_ref[...])


def _peer_call(h, wqt, keys, u, vt, g2, b2):
    n, d = h.shape
    nkeys = keys.shape[1]
    tm = min(_row_tile(n), 256)
    g = PEER_GROUP
    nc = nkeys // g
    hspec = pl.BlockSpec((tm, d), lambda i, c: (i, 0))
    full = lambda a: pl.BlockSpec(a.shape, lambda i, c: (0,) * a.ndim)
    assert nkeys == PEER_TOPK * SUBLANES and PEER_HEADS == SUBLANES
    dense = pltpu.VMEM((PEER_HEADS, nkeys, tm), F32)
    return pl.pallas_call(
        functools.partial(_peer_kernel, nc=nc), grid=(n // tm, nc),
        in_specs=[hspec, full(wqt), full(keys),
                  pl.BlockSpec((g * nkeys, d), lambda i, c: (c, 0)),
                  pl.BlockSpec((d, g * nkeys), lambda i, c: (0, c)), full(g2), full(b2)],
        out_specs=hspec, out_shape=jax.ShapeDtypeStruct((n, d), F32),
        scratch_shapes=[pltpu.VMEM((PEER_HEADS * 2 * PEER_HALF, tm), F32),
                        pltpu.VMEM((2 * PEER_HEADS, nkeys, tm), F32), dense, dense,
                        pltpu.VMEM((2, PEER_TOPK, SUBLANES, tm), F32), pltpu.VMEM((PEER_TOPK, SUBLANES, tm), F32),
                        pltpu.VMEM((4, SUBLANES, tm), F32), pltpu.VMEM((SUBLANES, tm), F32),
                        pltpu.VMEM((d, tm), F32), pltpu.VMEM((d, tm), BF16)],
        compiler_params=_cparams(("parallel", "arbitrary")), name="peer")(h, wqt, keys, u, vt, g2, b2)


def _round_up(x, m):
    return -(-x // m) * m


def _pad_rows(x, rows):
    return jnp.pad(x, ((0, 0), (0, rows - x.shape[1]), (0, 0)))


def _pad_flat(x, rows):
    return jnp.pad(x, ((0, rows - x.shape[0]), (0, 0)))


def kernel(x_prompt, x_sample, cache_k, cache_v, state_gla, meta, ln_in_g, ln_in_b, w_in, w_gk2, b_gk,
           gla_norm_g, w_pa, w_pb, w_o, ln1_g, ln1_b, peer_wq, peer_subkeys, peer_u, peer_v, ln2_g, ln2_b):
    bp, s_len, d = x_prompt.shape
    bd, t_new, _ = x_sample.shape
    past = cache_k.shape[2]
    n_len = N_META + s_len
    keep = min(s_len, past)
    lp = _round_up(n_len, ATTN_BLOCK)
    ts = _round_up(t_new, ATTN_BLOCK)
    tg = _round_up(t_new, GLA_CHUNK)
    tk_s = _round_up(N_META + past + t_new, ATTN_BLOCK)
    row = lambda a: a.reshape(1, -1)

    meta_rows = jnp.broadcast_to(meta[None].astype(x_prompt.dtype), (bp, N_META, d))
    xp = _pad_rows(jnp.concatenate([meta_rows, x_prompt], axis=1), lp).reshape(bp * lp, d)
    ns = bd * t_new
    ns_p = _round_up(ns, LANES)
    xs = _pad_flat(x_sample.reshape(ns, d), ns_p)
    hp = _ln_call(xp, ln_in_g, ln_in_b)
    hs = _ln_call(xs, ln_in_g, ln_in_b)

    outs = {k: [] for k in ("nk_p", "nv_p", "st_p", "nk_s", "nv_s", "st_s")}
    for l in range(DEPTH):
        glr0 = _SEC[7]
        w_main = jnp.concatenate([w_in[l][:, :glr0], w_in[l][:, glr0 + GLA_RANK:]], axis=1).astype(BF16)
        w_glr = jnp.pad(w_in[l][:, glr0:glr0 + GLA_RANK], ((0, 0), (0, LANES - GLA_RANK))).astype(BF16)
        w_gk2p = jnp.pad(w_gk2[l], ((0, LANES - GLA_RANK), (0, 0)))
        merge_w = (row(gla_norm_g[l]), w_pa[l].astype(BF16), w_pb[l].astype(BF16), w_o[l].astype(BF16),
                   row(ln1_g[l]), row(ln1_b[l]))
        peer_w = (peer_wq[l].T.astype(BF16), peer_subkeys[l], peer_u[l].astype(BF16),
                  peer_v[l].T.astype(BF16), row(ln2_g[l]), row(ln2_b[l]))

        qa, kab, vab, ka, va, qb, kb, vb, la, rb, ga, gb = _inproj_call(hp, w_main, w_glr, w_gk2p, row(b_gk[l]))
        b3 = lambda a: a.reshape(bp, lp, a.shape[-1])
        sb = _sb_attn_call(b3(qa), b3(kab), b3(vab), 0)
        s0t = jnp.zeros((bp, GLA_HEADS, GLA_DV, GLA_DK), F32)
        go, stp = _gla_call(b3(qb), b3(kb), b3(la), b3(vb), s0t, n_len)
        hp = _merge_call(hp, sb.reshape(bp * lp, SB_W), go.reshape(bp * lp, GLA_VW), rb, ga, gb, *merge_w)
        hp = _peer_call(hp, *peer_w)
        ka3 = ka.reshape(bp, lp, SB_HEADS, SB_DIM)
        va3 = va.reshape(bp, lp, SB_HEADS, SB_DIM)
        outs["nk_p"].append(ka3[:, n_len - keep:n_len])
        outs["nv_p"].append(va3[:, n_len - keep:n_len])
        outs["st_p"].append(jnp.swapaxes(stp, 2, 3).astype(state_gla.dtype))

        qa2, _, _, ka2, va2, qb2, kb2, vb2, la2, rb2, ga2, gb2 = _inproj_call(
            hs, w_main, w_glr, w_gk2p, row(b_gk[l]))
        s3 = lambda a: a[:ns].reshape(bd, t_new, a.shape[-1])
        flat = lambda a: _pad_flat(a[:, :t_new].reshape(ns, a.shape[-1]), ns_p)
        ka2n = s3(ka2)
        va2n = s3(va2)
        meta_k = jnp.broadcast_to(ka3[:1, :N_META].reshape(1, N_META, SB_W), (bd, N_META, SB_W))
        meta_v = jnp.broadcast_to(va3[:1, :N_META].reshape(1, N_META, SB_W), (bd, N_META, SB_W))
        k_all = jnp.concatenate([meta_k, cache_k[l].reshape(bd, past, SB_W), ka2n], axis=1)
        v_all = jnp.concatenate([meta_v, cache_v[l].reshape(bd, past, SB_W), va2n], axis=1)
        sb2 = _sb_attn_call(_pad_rows(s3(qa2), ts), _pad_rows(k_all, tk_s).astype(BF16),
                            _pad_rows(v_all, tk_s).astype(BF16), N_META + past)
        g3 = lambda a: _pad_rows(s3(a), tg)
        go2, sts = _gla_call(g3(qb2), g3(kb2), g3(la2), g3(vb2), jnp.swapaxes(state_gla[l], 2, 3), t_new)
        hs = _merge_call(hs, flat(sb2), flat(go2), rb2, ga2, gb2, *merge_w)
        hs = _peer_call(hs, *peer_w)
        outs["nk_s"].append(ka2n.reshape(bd, t_new, SB_HEADS, SB_DIM))
        outs["nv_s"].append(va2n.reshape(bd, t_new, SB_HEADS, SB_DIM))
        outs["st_s"].append(jnp.swapaxes(sts, 2, 3).astype(state_gla.dtype))

    y_prompt = hp.reshape(bp, lp, d)[:, N_META:n_len]
    y_sample = hs[:ns].reshape(bd, t_new, d)
    return (y_prompt, y_sample, jnp.stack(outs["nk_p"]), jnp.stack(outs["nv_p"]), jnp.stack(outs["st_p"]),
            jnp.stack(outs["nk_s"]), jnp.stack(outs["nv_s"]), jnp.stack(outs["st_s"]))
```

```python
import functools

import jax
import jax.numpy as jnp
from jax import lax
from jax.experimental import pallas as pl
from jax.experimental.pallas import tpu as pltpu

F32 = jnp.float32
BF16 = jnp.bfloat16

D_MODEL = 1024
DEPTH = 2
N_META = 16
SB_HEADS = 8
SB_DIM = 64
SB_W = SB_HEADS * SB_DIM
GLA_HEADS = 4
GLA_DK = 128
GLA_DV = 256
GLA_KW = GLA_HEADS * GLA_DK
GLA_VW = GLA_HEADS * GLA_DV
GLA_RANK = 16
GLA_TAU = 16.0
PEER_HEADS = 8
PEER_HALF = 128
PEER_TOPK = 16
DN_ALPHA = float((2 * DEPTH) ** 0.25)
LN_EPS = 1e-5
RMS_EPS = 1e-6

LANES = 128
SUBLANES = 8
VMEM_LIMIT_BYTES = 56 * 1024 * 1024

ATTN_BLOCK = 128
GLA_CHUNK = 128
PEER_GROUP = 32
EXP_ZERO_BELOW = -104.0

_NT = (((1,), (1,)), ((), ()))
_TN = (((0,), (0,)), ((), ()))


def _cparams(sem, flags=None):
    return pltpu.CompilerParams(dimension_semantics=sem, vmem_limit_bytes=VMEM_LIMIT_BYTES, flags=flags)


def _row_tile(n):
    for t in (512, 256, 128):
        if n % t == 0:
            return t
    raise ValueError(f"token count {n} is not a multiple of 128")


def _layer_norm(x, g, b):
    mu = jnp.mean(x, axis=-1, keepdims=True)
    xc = x - mu
    var = jnp.mean(xc * xc, axis=-1, keepdims=True)
    return xc * lax.rsqrt(var + LN_EPS) * g + b


def _softplus(x):
    return jnp.maximum(x, 0.0) + jnp.log(1.0 + jnp.exp(-jnp.abs(x)))


def _sigmoid(x):
    return 1.0 / (1.0 + jnp.exp(-x))


def _ln_kernel(x_ref, g_ref, b_ref, o_ref):
    o_ref[...] = _layer_norm(x_ref[...], g_ref[...], b_ref[...])


def _ln_call(x, g, b):
    n, d = x.shape
    tm = _row_tile(n)
    row = pl.BlockSpec((tm, d), lambda i: (i, 0))
    vec = pl.BlockSpec((1, d), lambda i: (0, 0))
    return pl.pallas_call(
        _ln_kernel, grid=(n // tm,), in_specs=[row, vec, vec], out_specs=row,
        out_shape=jax.ShapeDtypeStruct((n, d), F32), compiler_params=_cparams(("parallel",)),
        name="ln_in")(x, g.reshape(1, d), b.reshape(1, d))


_SEC = (0, 512, 1024, 1536, 2048, 2560, 3584, 4608, 5632, 6656)


def _inproj_kernel(h_ref, w_ref, wg_ref, wgk2_ref, bgk_ref,
                   qa_ref, kab_ref, vab_ref, ka_ref, va_ref, qb_ref, kb_ref, vb_ref, la_ref,
                   rb_ref, ga_ref, gb_ref):
    hb = h_ref[...].astype(BF16)

    def sec(i):
        return jnp.dot(hb, w_ref[:, _SEC[i]:_SEC[i + 1]], preferred_element_type=F32)

    qa_ref[...] = (sec(0) * (SB_DIM ** -0.5)).astype(BF16)
    ka = sec(1)
    ka_ref[...] = ka
    kab_ref[...] = ka.astype(BF16)
    va = sec(2)
    va_ref[...] = va
    vab_ref[...] = va.astype(BF16)
    qb_ref[...] = sec(3) * (GLA_DK ** -0.5)
    kb_ref[...] = sec(4)
    vb_ref[...] = sec(5).astype(BF16)
    rb = sec(6)
    rb_ref[...] = (rb * _sigmoid(rb)).astype(BF16)
    ga_ref[...] = _sigmoid(sec(7)).astype(BF16)
    gb_ref[...] = _sigmoid(sec(8)).astype(BF16)
    glr = jnp.dot(hb, wg_ref[...], preferred_element_type=F32)
    x = jnp.dot(glr, wgk2_ref[...], preferred_element_type=F32) + bgk_ref[...]
    la_ref[...] = -_softplus(-x) * (1.0 / GLA_TAU)


def _inproj_call(h, w_main, w_glr, w_gk2p, b_gk):
    n, d = h.shape
    tm = min(_row_tile(n), 256)
    row = lambda w: pl.BlockSpec((tm, w), lambda i: (i, 0))
    full = lambda a: pl.BlockSpec(a.shape, lambda i: (0,) * a.ndim)
    outs = [(SB_W, BF16), (SB_W, BF16), (SB_W, BF16), (SB_W, F32), (SB_W, F32),
            (GLA_KW, F32), (GLA_KW, F32), (GLA_VW, BF16), (GLA_KW, F32),
            (GLA_VW, BF16), (D_MODEL, BF16), (D_MODEL, BF16)]
    return pl.pallas_call(
        _inproj_kernel, grid=(n // tm,),
        in_specs=[row(d), full(w_main), full(w_glr), full(w_gk2p), full(b_gk)],
        out_specs=[row(w) for w, _ in outs],
        out_shape=[jax.ShapeDtypeStruct((n, w), t) for w, t in outs],
        compiler_params=_cparams(("parallel",)), name="in_proj")(h, w_main, w_glr, w_gk2p, b_gk)


def _sb_attn_kernel(q_ref, k_ref, v_ref, o_ref, carry_scr, acc_scr, *, q_off, nk):
    tq = tk = ATTN_BLOCK
    npair = SB_W // LANES
    i = pl.program_id(1)
    lane = lax.broadcasted_iota(jnp.int32, (tq, LANES), 1)
    r_iota = lax.broadcasted_iota(jnp.int32, (tq, tk), 0)
    c_iota = lax.broadcasted_iota(jnp.int32, (tq, tk), 1)
    later_ones = jnp.concatenate([(r_iota > c_iota).astype(BF16), jnp.ones((tk, tk), BF16)], axis=1)
    qpos = q_off + i * tq + r_iota
    j_start = jnp.minimum((q_off + i * tq + tq - 2) // tk, nk - 1)
    carry_scr[...] = jnp.zeros_like(carry_scr)
    acc_scr[...] = jnp.zeros_like(acc_scr)

    def cond(c):
        j, cmax = c
        return jnp.logical_and(j >= 0, cmax > EXP_ZERO_BELOW)

    def body(c):
        j, _ = c
        off = pl.multiple_of(j * tk, tk)
        mask = (off + c_iota) < qpos
        heads = range(SB_HEADS)
        zs = []
        for h in heads:
            cols = slice((h // 2) * LANES, (h // 2 + 1) * LANES)
            q = q_ref[0, :, cols]
            qm = jnp.where(lane < SB_DIM if h % 2 == 0 else lane >= SB_DIM, q, jnp.zeros_like(q))
            zs.append(lax.dot_general(qm, k_ref[0, pl.ds(off, tk), cols], _NT, preferred_element_type=F32))
        lbs, css = [], []
        for h in heads:
            sp = _softplus(zs[h])
            lbs.append(zs[h] - sp)
            lk = jnp.where(mask, -sp, 0.0)
            hi = lk.astype(BF16)
            lo = (lk - hi.astype(F32)).astype(BF16)
            css.append(jnp.dot(hi, later_ones, preferred_element_type=F32)
                       + jnp.dot(lo, later_ones, preferred_element_type=F32))
        cmax = None
        pvs = []
        for h in heads:
            cols = slice((h // 2) * LANES, (h // 2 + 1) * LANES)
            carry = carry_scr[h]
            w = jnp.where(mask, jnp.exp(lbs[h] + css[h][:, :tk] + carry), 0.0).astype(BF16)
            pvs.append(jnp.dot(w, v_ref[0, pl.ds(off, tk), cols], preferred_element_type=F32))
            carry = carry + css[h][:, tk:]
            carry_scr[h] = carry
            cmax = carry if cmax is None else jnp.maximum(cmax, carry)
        for p in range(npair):
            acc_scr[p] += jnp.where(lane < SB_DIM, pvs[2 * p], pvs[2 * p + 1])
        return j - 1, jnp.max(cmax)

    lax.while_loop(cond, body, (j_start, jnp.float32(0.0)))
    for p in range(npair):
        o_ref[0, :, p * LANES:(p + 1) * LANES] = acc_scr[p].astype(o_ref.dtype)


def _sb_attn_call(q, k, v, q_off):
    b, tq_all, _ = q.shape
    tk_all = k.shape[1]
    nq, nk = tq_all // ATTN_BLOCK, tk_all // ATTN_BLOCK
    qspec = pl.BlockSpec((1, ATTN_BLOCK, SB_W), lambda bi, i: (bi, i, 0))
    kspec = pl.BlockSpec((1, tk_all, SB_W), lambda bi, i: (bi, 0, 0))
    return pl.pallas_call(
        functools.partial(_sb_attn_kernel, q_off=q_off, nk=nk),
        grid=(b, nq), in_specs=[qspec, kspec, kspec], out_specs=qspec,
        out_shape=jax.ShapeDtypeStruct((b, tq_all, SB_W), BF16),
        scratch_shapes=[pltpu.VMEM((SB_HEADS, ATTN_BLOCK, ATTN_BLOCK), F32),
                        pltpu.VMEM((SB_W // LANES, ATTN_BLOCK, LANES), F32)],
        compiler_params=_cparams(("parallel", "arbitrary")), name="sb_attn")(q, k, v)


def _half_sizes(c):
    out, m = [], c // 2
    while m >= 1:
        out.append(m)
        m //= 2
    return out


def _block_ref_rows(b, m):
    c, dk = b.shape
    if 2 * m >= SUBLANES:
        pieces = [jnp.broadcast_to(b[a + m - 1:a + m, :], (2 * m, dk)) for a in range(0, c, 2 * m)]
        return pieces[0] if len(pieces) == 1 else jnp.concatenate(pieces, axis=0)
    row8 = lax.broadcasted_iota(jnp.int32, (SUBLANES, dk), 0)
    tiles = []
    for a in range(0, c, SUBLANES):
        tile = None
        for off in range(0, SUBLANES, 2 * m):
            r = jnp.broadcast_to(b[a + off + m - 1:a + off + m, :], (SUBLANES, dk))
            tile = r if tile is None else jnp.where(row8 >= off, r, tile)
        tiles.append(tile)
    return jnp.concatenate(tiles, axis=0)


def _gla_kernel(q_ref, k_ref, la_ref, v_ref, s0_ref, o_ref, st_ref, *, n_valid):
    c = GLA_CHUNK
    ci = pl.program_id(1)

    @pl.when(ci == 0)
    def _():
        st_ref[...] = s0_ref[...]

    r_iota = lax.broadcasted_iota(jnp.int32, (c, c), 0)
    c_iota = lax.broadcasted_iota(jnp.int32, (c, c), 1)
    tril = (r_iota >= c_iota).astype(BF16)
    level = jnp.where(c_iota < r_iota, 31 - lax.clz(r_iota ^ c_iota), jnp.where(c_iota == r_iota, -1, -2))
    valid = (ci * c + lax.broadcasted_iota(jnp.int32, (c, GLA_DK), 0)) < n_valid

    heads = range(GLA_HEADS)
    ks = [slice(h * GLA_DK, (h + 1) * GLA_DK) for h in heads]
    vs = [slice(h * GLA_DV, (h + 1) * GLA_DV) for h in heads]
    k = [jnp.where(valid, k_ref[0, :, ks[h]], 0.0) for h in heads]
    q = [q_ref[0, :, ks[h]] for h in heads]
    b = []
    for h in heads:
        la = jnp.where(valid, la_ref[0, :, ks[h]], 0.0)
        l1 = la.astype(BF16)
        r1 = la - l1.astype(F32)
        l2 = r1.astype(BF16)
        l3 = (r1 - l2.astype(F32)).astype(BF16)
        b.append(jnp.dot(tril, l1, preferred_element_type=F32) + jnp.dot(tril, l2, preferred_element_type=F32)
                 + jnp.dot(tril, l3, preferred_element_type=F32))
    inter = [lax.dot_general((q[h] * jnp.exp(b[h])).astype(BF16), st_ref[0, h].astype(BF16), _NT,
                             preferred_element_type=F32) for h in heads]
    scores = [jnp.where(level == -1, lax.dot_general(q[h].astype(BF16), k[h].astype(BF16), _NT,
                                                     preferred_element_type=F32), 0.0) for h in heads]
    for m in _half_sizes(c):
        for h in heads:
            ref_rows = _block_ref_rows(b[h], m)
            qm = (q[h] * jnp.exp(jnp.minimum(b[h] - ref_rows, 0.0))).astype(BF16)
            km = (k[h] * jnp.exp(jnp.minimum(ref_rows - b[h], 0.0))).astype(BF16)
            scores[h] = jnp.where(level == m.bit_length() - 1,
                                  lax.dot_general(qm, km, _NT, preferred_element_type=F32), scores[h])
    for h in heads:
        v = v_ref[0, :, vs[h]]
        o_ref[0, :, vs[h]] = inter[h] + jnp.dot(scores[h].astype(BF16), v, preferred_element_type=F32)
        b_last = b[h][c - 1:c, :]
        kd = (k[h] * jnp.exp(b_last - b[h])).astype(BF16)
        st_ref[0, h] = st_ref[0, h] * jnp.exp(b_last) + lax.dot_general(v, kd, _TN, preferred_element_type=F32)


def _gla_call(q, k, la, v, s0t, n_valid):
    b, t, _ = q.shape
    c = GLA_CHUNK
    kspec = pl.BlockSpec((1, c, GLA_KW), lambda bi, ci: (bi, ci, 0))
    vspec = pl.BlockSpec((1, c, GLA_VW), lambda bi, ci: (bi, ci, 0))
    sspec = pl.BlockSpec((1, GLA_HEADS, GLA_DV, GLA_DK), lambda bi, ci: (bi, 0, 0, 0))
    return pl.pallas_call(
        functools.partial(_gla_kernel, n_valid=n_valid),
        grid=(b, t // c), in_specs=[kspec, kspec, kspec, vspec, sspec],
        out_specs=[vspec, sspec],
        out_shape=[jax.ShapeDtypeStruct((b, t, GLA_VW), F32),
                   jax.ShapeDtypeStruct((b, GLA_HEADS, GLA_DV, GLA_DK), F32)],
        compiler_params=_cparams(("parallel", "arbitrary")), name="gla")(q, k, la, v, s0t)


def _merge_kernel(h_ref, sb_ref, go_ref, rb_ref, ga_ref, gb_ref, gn_ref, wpa_ref, wpb_ref, wo_ref,
                  g1_ref, b1_ref, out_ref):
    go = go_ref[...]
    parts = []
    for h in range(GLA_HEADS):
        x = go[:, h * GLA_DV:(h + 1) * GLA_DV]
        ms = jnp.mean(x * x, axis=-1, keepdims=True)
        parts.append(x * lax.rsqrt(ms + RMS_EPS) * gn_ref[...])
    o = jnp.concatenate(parts, axis=-1)
    gated = (o * rb_ref[...]).astype(BF16)
    gla_y = jnp.dot(gated, wpb_ref[...], preferred_element_type=F32)
    sb_y = jnp.dot(sb_ref[...], wpa_ref[...], preferred_element_type=F32)
    merged = ga_ref[...] * sb_y + gb_ref[...] * gla_y
    y = jnp.dot(merged.astype(BF16), wo_ref[...], preferred_element_type=F32)
    out_ref[...] = _layer_norm(DN_ALPHA * h_ref[...] + y, g1_ref[...], b1_ref[...])


def _merge_call(h, sb, go, rb, ga, gb, gn, wpa, wpb, wo, g1, b1):
    n, d = h.shape
    tm = min(_row_tile(n), 256)
    row = lambda w: pl.BlockSpec((tm, w), lambda i: (i, 0))
    full = lambda a: pl.BlockSpec(a.shape, lambda i: (0,) * a.ndim)
    args = (h, sb, go, rb, ga, gb, gn, wpa, wpb, wo, g1, b1)
    return pl.pallas_call(
        _merge_kernel, grid=(n // tm,),
        in_specs=[row(d), row(SB_W), row(GLA_VW), row(GLA_VW), row(d), row(d)] + [full(a) for a in args[6:]],
        out_specs=row(d), out_shape=jax.ShapeDtypeStruct((n, d), F32),
        compiler_params=_cparams(("parallel",)), name="merge")(*args)


def _sort_network(n):
    def merge(lo, hi, r):
        step = r * 2
        if step < hi - lo:
            yield from merge(lo, hi, step)
            yield from merge(lo + r, hi, step)
            yield from [(i, i + r) for i in range(lo + r, hi - r, step)]
        else:
            yield (lo, lo + r)

    def sort(lo, hi):
        if hi - lo >= 1:
            mid = lo + (hi - lo) // 2
            yield from sort(lo, mid)
            yield from sort(mid + 1, hi)
            yield from merge(lo, hi, 1)

    return tuple(sort(0, n - 1))


def _bitonic_merge_network(n):
    out, k = [], n // 2
    while k >= 1:
        out.extend((i, i + k) for i in range(n) if not i & k)
        k //= 2
    return tuple(out)


_SORT_TOPK = _sort_network(PEER_TOPK)
_MERGE_TOPK = _bitonic_merge_network(PEER_TOPK)


def _exchange(v, net):
    v = list(v)
    for i, j in net:
        hi, lo = jnp.maximum(v[i], v[j]), jnp.minimum(v[i], v[j])
        v[i], v[j] = hi, lo
    return v


def _merge_top(x, y):
    n = PEER_TOPK
    return _exchange([jnp.maximum(x[k], y[n - 1 - k]) for k in range(n)], _MERGE_TOPK)


def _exchange_lockstep(vs, net):
    vs = [list(v) for v in vs]
    for i, j in net:
        for v in vs:
            hi, lo = jnp.maximum(v[i], v[j]), jnp.minimum(v[i], v[j])
            v[i], v[j] = hi, lo
    return vs


def _column_top(ss):
    n = PEER_TOPK
    vs = _exchange_lockstep([[s[SUBLANES * k:SUBLANES * (k + 1), :] for k in range(n)] for s in ss], _SORT_TOPK)
    for shift in (4, 2, 1):
        vs = _exchange_lockstep([[jnp.maximum(v[k], pltpu.roll(v[n - 1 - k], shift, 0)) for k in range(n)]
                                 for v in vs], _MERGE_TOPK)
    return vs


_CAND_COLS = tuple(PEER_TOPK // (a + 1) for a in range(PEER_TOPK // 2))


def _ranked_top(s):
    rows = s.shape[0]
    iota = lax.broadcasted_iota(jnp.int32, s.shape, 0)
    rank = jnp.full(s.shape, float(PEER_TOPK), F32)
    vals = []
    for k in range(PEER_TOPK):
        m = jnp.max(s, axis=0, keepdims=True)
        first = jnp.min(jnp.where(s == m, iota, rows), axis=0, keepdims=True)
        hit = iota == first
        rank = jnp.where(hit, float(k), rank)
        s = jnp.where(hit, -jnp.inf, s)
        vals.append(m)
    return rank, vals


def _peer_kernel(h_ref, wqt_ref, keys_ref, u_ref, vt_ref, g2_ref, b2_ref, out_ref,
                 qt_scr, s_scr, e1_scr, e2_scr, top_scr, th_scr, stat_scr, flag_scr, acc_scr, hbt_scr, *, nc):
    ci = pl.program_id(1)
    tm = h_ref.shape[0]
    nkeys = keys_ref.shape[1]

    @pl.when(ci == 0)
    def _():
        hbt_scr[...] = h_ref[...].T.astype(BF16)
        qt_scr[...] = jnp.dot(wqt_ref[...], hbt_scr[...], preferred_element_type=F32)
        top_scr[...] = jnp.zeros_like(top_scr)
        flag_scr[...] = jnp.zeros_like(flag_scr)
        sub = lax.broadcasted_iota(jnp.int32, (SUBLANES, tm), 0)

        def per_head_scores(hd, carry):
            base = pl.multiple_of(hd * (2 * PEER_HALF), 2 * PEER_HALF)
            ss = [jnp.dot(keys_ref[side], qt_scr[pl.ds(base + side * PEER_HALF, PEER_HALF), :],
                          preferred_element_type=F32) for side in range(2)]
            tops = _column_top(ss)
            tie = flag_scr[...]
            for side in range(2):
                s, top = ss[side], tops[side]
                s_scr[2 * hd + side] = s
                at_least = jnp.sum(jnp.where(s >= top[PEER_TOPK - 1][0:1, :], 1.0, 0.0), axis=0, keepdims=True)
                tie = jnp.where(at_least != float(PEER_TOPK), 1.0, tie)
                for k in range(PEER_TOPK):
                    top_scr[side, k] = jnp.where(sub == hd, top[k], top_scr[side, k])
                    if k:
                        tie = jnp.where(top[k - 1] == top[k], 1.0, tie)
            flag_scr[...] = tie
            return carry

        lax.fori_loop(0, PEER_HEADS, per_head_scores, 0)

        t1 = [top_scr[0, k] for k in range(PEER_TOPK)]
        t2 = [top_scr[1, k] for k in range(PEER_TOPK)]
        neg = jnp.full((SUBLANES, tm), -jnp.inf, F32)
        lists = [[t1[a] + t2[b] for b in range(nb)] + [neg] * (PEER_TOPK - nb) for a, nb in enumerate(_CAND_COLS)]
        half = PEER_TOPK // 2
        lists.append([t1[a] + t2[0] for a in range(half, PEER_TOPK)] + [neg] * half)
        while len(lists) > 1:
            lists = [_merge_top(lists[i], lists[i + 1]) if i + 1 < len(lists) else lists[i]
                     for i in range(0, len(lists), 2)]
        cv = lists[0]
        tau = cv[PEER_TOPK - 1]
        zsum = jnp.exp(cv[0] - cv[0])
        for k in range(1, PEER_TOPK):
            zsum = zsum + jnp.exp(cv[k] - cv[0])
        stat_scr[1] = zsum
        stat_scr[2] = t1[0]
        stat_scr[3] = t2[0]
        n_reach = jnp.zeros((SUBLANES, tm), F32)
        for a in range(PEER_TOPK):
            th = jnp.full((SUBLANES, tm), jnp.inf, F32)
            for b in range(PEER_TOPK // (a + 1)):
                reach = t1[a] + t2[b] >= tau
                th = jnp.where(reach, t2[b], th)
                n_reach = n_reach + jnp.where(reach, 1.0, 0.0)
            th_scr[a] = th
        any_tie = jnp.max(jnp.where(n_reach != float(PEER_TOPK), 1.0, flag_scr[...])) > 0.0

        @pl.when(jnp.logical_not(any_tie))
        def _():
            def per_head(hd, carry):
                zr = stat_scr[1, pl.ds(hd, 1), :]
                m1 = stat_scr[2, pl.ds(hd, 1), :]
                m2 = stat_scr[3, pl.ds(hd, 1), :]
                s1 = s_scr[2 * hd]
                e1_scr[hd] = jnp.exp(s1 - m1) * (0.5 / zr)
                e2_scr[hd] = jnp.exp(s_scr[2 * hd + 1] - m2)
                thr = jnp.full((nkeys, tm), jnp.inf, F32)
                for a in range(PEER_TOPK):
                    thr = jnp.where(s1 == top_scr[0, a, pl.ds(hd, 1), :], th_scr[a, pl.ds(hd, 1), :], thr)
                s_scr[2 * hd] = thr
                return carry

            lax.fori_loop(0, PEER_HEADS, per_head, 0)

        @pl.when(any_tie)
        def _():
            def per_head(hd, carry):
                s1 = s_scr[2 * hd]
                s2 = s_scr[2 * hd + 1]
                rank1, v1 = _ranked_top(s1)
                rank2, v2 = _ranked_top(s2)
                v2col = jnp.concatenate(v2, axis=0)
                cand = jnp.concatenate([v1[a] + v2col for a in range(PEER_TOPK)], axis=0)
                crank, cvals = _ranked_top(cand)
                taken = jnp.where(crank < float(PEER_TOPK), 1.0, 0.0)
                pairs = jnp.zeros((nkeys, tm), F32)
                for a in range(PEER_TOPK):
                    n_a = jnp.sum(taken[a * PEER_TOPK:(a + 1) * PEER_TOPK, :], axis=0, keepdims=True)
                    pairs = jnp.where(rank1 == float(a), n_a, pairs)
                zs = jnp.exp(cvals[0] - cvals[0])
                for k in range(1, PEER_TOPK):
                    zs = zs + jnp.exp(cvals[k] - cvals[0])
                e1_scr[hd] = jnp.exp(s1 - v1[0]) * (0.5 / zs)
                e2_scr[hd] = jnp.exp(s2 - v2[0])
                s_scr[2 * hd] = jnp.where(pairs > 0.0, 1.0 - pairs, jnp.inf)
                s_scr[2 * hd + 1] = jnp.where(rank2 < float(PEER_TOPK), -rank2, -jnp.inf)
                return carry

            lax.fori_loop(0, PEER_HEADS, per_head, 0)

        acc_scr[...] = jnp.zeros_like(acc_scr)

    piece = 2 * nkeys
    npiece = PEER_GROUP // 2

    def gate_rows(i1):
        w = None
        for hd in range(PEER_HEADS):
            thr = s_scr[2 * hd, pl.ds(i1, 1), :]
            e1r = e1_scr[hd, pl.ds(i1, 1), :]
            t = jnp.where(s_scr[2 * hd + 1] >= thr, e2_scr[hd] * e1r, 0.0)
            w = t if w is None else w + t
        return w

    for pi in range(npiece):
        rows = slice(pi * piece, (pi + 1) * piece)
        pre = jnp.dot(u_ref[rows, :], hbt_scr[...], preferred_element_type=F32)
        i1 = ci * PEER_GROUP + 2 * pi
        w = jnp.concatenate([gate_rows(i1), gate_rows(i1 + 1)], axis=0)
        act = pre * (1.0 + lax.erf(pre * (2.0 ** -0.5)))
        a = (act * w).astype(BF16)
        acc_scr[...] += jnp.dot(vt_ref[:, rows], a, preferred_element_type=F32)

    @pl.when(ci == nc - 1)
    def _():
        y = acc_scr[...].T
        out_ref[...] = _layer_norm(DN_ALPHA * h_ref[...] + y, g2_ref[...], b2_ref[...])


def _peer_call(h, wqt, keys, u, vt, g2, b2):
    n, d = h.shape
    nkeys = keys.shape[1]
    tm = min(_row_tile(n), 256)
    g = PEER_GROUP
    nc = nkeys // g
    hspec = pl.BlockSpec((tm, d), lambda i, c: (i, 0))
    full = lambda a: pl.BlockSpec(a.shape, lambda i, c: (0,) * a.ndim)
    assert nkeys == PEER_TOPK * SUBLANES and PEER_HEADS == SUBLANES
    dense = pltpu.VMEM((PEER_HEADS, nkeys, tm), F32)
    return pl.pallas_call(
        functools.partial(_peer_kernel, nc=nc), grid=(n // tm, nc),
        in_specs=[hspec, full(wqt), full(keys),
                  pl.BlockSpec((g * nkeys, d), lambda i, c: (c, 0)),
                  pl.BlockSpec((d, g * nkeys), lambda i, c: (0, c)), full(g2), full(b2)],
        out_specs=hspec, out_shape=jax.ShapeDtypeStruct((n, d), F32),
        scratch_shapes=[pltpu.VMEM((PEER_HEADS * 2 * PEER_HALF, tm), F32),
                        pltpu.VMEM((2 * PEER_HEADS, nkeys, tm), F32), dense, dense,
                        pltpu.VMEM((2, PEER_TOPK, SUBLANES, tm), F32), pltpu.VMEM((PEER_TOPK, SUBLANES, tm), F32),
                        pltpu.VMEM((4, SUBLANES, tm), F32), pltpu.VMEM((SUBLANES, tm), F32),
                        pltpu.VMEM((d, tm), F32), pltpu.VMEM((d, tm), BF16)],
        compiler_params=_cparams(("parallel", "arbitrary")), name="peer")(h, wqt, keys, u, vt, g2, b2)


def _round_up(x, m):
    return -(-x // m) * m


def _pad_rows(x, rows):
    return jnp.pad(x, ((0, 0), (0, rows - x.shape[1]), (0, 0)))


def _pad_flat(x, rows):
    return jnp.pad(x, ((0, rows - x.shape[0]), (0, 0)))


def kernel(x_prompt, x_sample, cache_k, cache_v, state_gla, meta, ln_in_g, ln_in_b, w_in, w_gk2, b_gk,
           gla_norm_g, w_pa, w_pb, w_o, ln1_g, ln1_b, peer_wq, peer_subkeys, peer_u, peer_v, ln2_g, ln2_b):
    bp, s_len, d = x_prompt.shape
    bd, t_new, _ = x_sample.shape
    past = cache_k.shape[2]
    n_len = N_META + s_len
    keep = min(s_len, past)
    lp = _round_up(n_len, ATTN_BLOCK)
    ts = _round_up(t_new, ATTN_BLOCK)
    tg = _round_up(t_new, GLA_CHUNK)
    tk_s = _round_up(N_META + past + t_new, ATTN_BLOCK)
    row = lambda a: a.reshape(1, -1)

    meta_rows = jnp.broadcast_to(meta[None].astype(x_prompt.dtype), (bp, N_META, d))
    xp = _pad_rows(jnp.concatenate([meta_rows, x_prompt], axis=1), lp).reshape(bp * lp, d)
    ns = bd * t_new
    ns_p = _round_up(ns, LANES)
    xs = _pad_flat(x_sample.reshape(ns, d), ns_p)
    hp = _ln_call(xp, ln_in_g, ln_in_b)
    hs = _ln_call(xs, ln_in_g, ln_in_b)

    outs = {k: [] for k in ("nk_p", "nv_p", "st_p", "nk_s", "nv_s", "st_s")}
    for l in range(DEPTH):
        glr0 = _SEC[7]
        w_main = jnp.concatenate([w_in[l][:, :glr0], w_in[l][:, glr0 + GLA_RANK:]], axis=1).astype(BF16)
        w_glr = jnp.pad(w_in[l][:, glr0:glr0 + GLA_RANK], ((0, 0), (0, LANES - GLA_RANK))).astype(BF16)
        w_gk2p = jnp.pad(w_gk2[l], ((0, LANES - GLA_RANK), (0, 0)))
        merge_w = (row(gla_norm_g[l]), w_pa[l].astype(BF16), w_pb[l].astype(BF16), w_o[l].astype(BF16),
                   row(ln1_g[l]), row(ln1_b[l]))
        peer_w = (peer_wq[l].T.astype(BF16), peer_subkeys[l], peer_u[l].astype(BF16),
                  peer_v[l].T.astype(BF16), row(ln2_g[l]), row(ln2_b[l]))

        qa, kab, vab, ka, va, qb, kb, vb, la, rb, ga, gb = _inproj_call(hp, w_main, w_glr, w_gk2p, row(b_gk[l]))
        b3 = lambda a: a.reshape(bp, lp, a.shape[-1])
        sb = _sb_attn_call(b3(qa), b3(kab), b3(vab), 0)
        s0t = jnp.zeros((bp, GLA_HEADS, GLA_DV, GLA_DK), F32)
        go, stp = _gla_call(b3(qb), b3(kb), b3(la), b3(vb), s0t, n_len)
        hp = _merge_call(hp, sb.reshape(bp * lp, SB_W), go.reshape(bp * lp, GLA_VW), rb, ga, gb, *merge_w)
        hp = _peer_call(hp, *peer_w)
        ka3 = ka.reshape(bp, lp, SB_HEADS, SB_DIM)
        va3 = va.reshape(bp, lp, SB_HEADS, SB_DIM)
        outs["nk_p"].append(ka3[:, n_len - keep:n_len])
        outs["nv_p"].append(va3[:, n_len - keep:n_len])
        outs["st_p"].append(jnp.swapaxes(stp, 2, 3).astype(state_gla.dtype))

        qa2, _, _, ka2, va2, qb2, kb2, vb2, la2, rb2, ga2, gb2 = _inproj_call(
            hs, w_main, w_glr, w_gk2p, row(b_gk[l]))
        s3 = lambda a: a[:ns].reshape(bd, t_new, a.shape[-1])
        flat = lambda a: _pad_flat(a[:, :t_new].reshape(ns, a.shape[-1]), ns_p)
        ka2n = s3(ka2)
        va2n = s3(va2)
        meta_k = jnp.broadcast_to(ka3[:1, :N_META].reshape(1, N_META, SB_W), (bd, N_META, SB_W))
        meta_v = jnp.broadcast_to(va3[:1, :N_META].reshape(1, N_META, SB_W), (bd, N_META, SB_W))
        k_all = jnp.concatenate([meta_k, cache_k[l].reshape(bd, past, SB_W), ka2n], axis=1)
        v_all = jnp.concatenate([meta_v, cache_v[l].reshape(bd, past, SB_W), va2n], axis=1)
        sb2 = _sb_attn_call(_pad_rows(s3(qa2), ts), _pad_rows(k_all, tk_s).astype(BF16),
                            _pad_rows(v_all, tk_s).astype(BF16), N_META + past)
        g3 = lambda a: _pad_rows(s3(a), tg)
        go2, sts = _gla_call(g3(qb2), g3(kb2), g3(la2), g3(vb2), jnp.swapaxes(state_gla[l], 2, 3), t_new)
        hs = _merge_call(hs, flat(sb2), flat(go2), rb2, ga2, gb2, *merge_w)
        hs = _peer_call(hs, *peer_w)
        outs["nk_s"].append(ka2n.reshape(bd, t_new, SB_HEADS, SB_DIM))
        outs["nv_s"].append(va2n.reshape(bd, t_new, SB_HEADS, SB_DIM))
        outs["st_s"].append(jnp.swapaxes(sts, 2, 3).astype(state_gla.dtype))

    y_prompt = hp.reshape(bp, lp, d)[:, N_META:n_len]
    y_sample = hs[:ns].reshape(bd, t_new, d)
    return (y_prompt, y_sample, jnp.stack(outs["nk_p"]), jnp.stack(outs["nv_p"]), jnp.stack(outs["st_p"]),
            jnp.stack(outs["nk_s"]), jnp.stack(outs["nv_s"]), jnp.stack(outs["st_s"]))
```

```python
import functools

import jax
import jax.numpy as jnp
from jax import lax
from jax.experimental import pallas as pl
from jax.experimental.pallas import tpu as pltpu

F32 = jnp.float32
BF16 = jnp.bfloat16

D_MODEL = 1024
DEPTH = 2
N_META = 16
SB_HEADS = 8
SB_DIM = 64
SB_W = SB_HEADS * SB_DIM
GLA_HEADS = 4
GLA_DK = 128
GLA_DV = 256
GLA_KW = GLA_HEADS * GLA_DK
GLA_VW = GLA_HEADS * GLA_DV
GLA_RANK = 16
GLA_TAU = 16.0
PEER_HEADS = 8
PEER_HALF = 128
PEER_TOPK = 16
DN_ALPHA = float((2 * DEPTH) ** 0.25)
LN_EPS = 1e-5
RMS_EPS = 1e-6

LANES = 128
SUBLANES = 8
VMEM_LIMIT_BYTES = 56 * 1024 * 1024

ATTN_BLOCK = 128
GLA_CHUNK = 128
PEER_GROUP = 32
EXP_ZERO_BELOW = -104.0

_NT = (((1,), (1,)), ((), ()))
_TN = (((0,), (0,)), ((), ()))


def _cparams(sem, flags=None):
    return pltpu.CompilerParams(dimension_semantics=sem, vmem_limit_bytes=VMEM_LIMIT_BYTES, flags=flags)


def _row_tile(n):
    for t in (512, 256, 128):
        if n % t == 0:
            return t
    raise ValueError(f"token count {n} is not a multiple of 128")


def _layer_norm(x, g, b):
    mu = jnp.mean(x, axis=-1, keepdims=True)
    xc = x - mu
    var = jnp.mean(xc * xc, axis=-1, keepdims=True)
    return xc * lax.rsqrt(var + LN_EPS) * g + b


def _softplus(x):
    return jnp.maximum(x, 0.0) + jnp.log(1.0 + jnp.exp(-jnp.abs(x)))


def _sigmoid(x):
    return 1.0 / (1.0 + jnp.exp(-x))


def _ln_kernel(x_ref, g_ref, b_ref, o_ref):
    o_ref[...] = _layer_norm(x_ref[...], g_ref[...], b_ref[...])


def _ln_call(x, g, b):
    n, d = x.shape
    tm = _row_tile(n)
    row = pl.BlockSpec((tm, d), lambda i: (i, 0))
    vec = pl.BlockSpec((1, d), lambda i: (0, 0))
    return pl.pallas_call(
        _ln_kernel, grid=(n // tm,), in_specs=[row, vec, vec], out_specs=row,
        out_shape=jax.ShapeDtypeStruct((n, d), F32), compiler_params=_cparams(("parallel",)),
        name="ln_in")(x, g.reshape(1, d), b.reshape(1, d))


_SEC = (0, 512, 1024, 1536, 2048, 2560, 3584, 4608, 5632, 6656)


def _inproj_kernel(h_ref, w_ref, wg_ref, wgk2_ref, bgk_ref,
                   qa_ref, kab_ref, vab_ref, ka_ref, va_ref, qb_ref, kb_ref, vb_ref, la_ref,
                   rb_ref, ga_ref, gb_ref):
    hb = h_ref[...].astype(BF16)

    def sec(i):
        return jnp.dot(hb, w_ref[:, _SEC[i]:_SEC[i + 1]], preferred_element_type=F32)

    qa_ref[...] = (sec(0) * (SB_DIM ** -0.5)).astype(BF16)
    ka = sec(1)
    ka_ref[...] = ka
    kab_ref[...] = ka.astype(BF16)
    va = sec(2)
    va_ref[...] = va
    vab_ref[...] = va.astype(BF16)
    qb_ref[...] = sec(3) * (GLA_DK ** -0.5)
    kb_ref[...] = sec(4)
    vb_ref[...] = sec(5).astype(BF16)
    rb = sec(6)
    rb_ref[...] = (rb * _sigmoid(rb)).astype(BF16)
    ga_ref[...] = _sigmoid(sec(7)).astype(BF16)
    gb_ref[...] = _sigmoid(sec(8)).astype(BF16)
    glr = jnp.dot(hb, wg_ref[...], preferred_element_type=F32)
    x = jnp.dot(glr, wgk2_ref[...], preferred_element_type=F32) + bgk_ref[...]
    la_ref[...] = -_softplus(-x) * (1.0 / GLA_TAU)


def _inproj_call(h, w_main, w_glr, w_gk2p, b_gk):
    n, d = h.shape
    tm = min(_row_tile(n), 256)
    row = lambda w: pl.BlockSpec((tm, w), lambda i: (i, 0))
    full = lambda a: pl.BlockSpec(a.shape, lambda i: (0,) * a.ndim)
    outs = [(SB_W, BF16), (SB_W, BF16), (SB_W, BF16), (SB_W, F32), (SB_W, F32),
            (GLA_KW, F32), (GLA_KW, F32), (GLA_VW, BF16), (GLA_KW, F32),
            (GLA_VW, BF16), (D_MODEL, BF16), (D_MODEL, BF16)]
    return pl.pallas_call(
        _inproj_kernel, grid=(n // tm,),
        in_specs=[row(d), full(w_main), full(w_glr), full(w_gk2p), full(b_gk)],
        out_specs=[row(w) for w, _ in outs],
        out_shape=[jax.ShapeDtypeStruct((n, w), t) for w, t in outs],
        compiler_params=_cparams(("parallel",)), name="in_proj")(h, w_main, w_glr, w_gk2p, b_gk)


def _sb_attn_kernel(q_ref, k_ref, v_ref, o_ref, carry_scr, acc_scr, *, q_off, nk):
    tq = tk = ATTN_BLOCK
    npair = SB_W // LANES
    i = pl.program_id(1)
    lane = lax.broadcasted_iota(jnp.int32, (tq, LANES), 1)
    r_iota = lax.broadcasted_iota(jnp.int32, (tq, tk), 0)
    c_iota = lax.broadcasted_iota(jnp.int32, (tq, tk), 1)
    later_ones = jnp.concatenate([(r_iota > c_iota).astype(BF16), jnp.ones((tk, tk), BF16)], axis=1)
    qpos = q_off + i * tq + r_iota
    j_start = jnp.minimum((q_off + i * tq + tq - 2) // tk, nk - 1)
    carry_scr[...] = jnp.zeros_like(carry_scr)
    acc_scr[...] = jnp.zeros_like(acc_scr)

    def cond(c):
        j, cmax = c
        return jnp.logical_and(j >= 0, cmax > EXP_ZERO_BELOW)

    def body(c):
        j, _ = c
        off = pl.multiple_of(j * tk, tk)
        mask = (off + c_iota) < qpos
        heads = range(SB_HEADS)
        zs = []
        for h in heads:
            cols = slice((h // 2) * LANES, (h // 2 + 1) * LANES)
            q = q_ref[0, :, cols]
            qm = jnp.where(lane < SB_DIM if h % 2 == 0 else lane >= SB_DIM, q, jnp.zeros_like(q))
            zs.append(lax.dot_general(qm, k_ref[0, pl.ds(off, tk), cols], _NT, preferred_element_type=F32))
        lbs, css = [], []
        for h in heads:
            sp = _softplus(zs[h])
            lbs.append(zs[h] - sp)
            lk = jnp.where(mask, -sp, 0.0)
            hi = lk.astype(BF16)
            lo = (lk - hi.astype(F32)).astype(BF16)
            css.append(jnp.dot(hi, later_ones, preferred_element_type=F32)
                       + jnp.dot(lo, later_ones, preferred_element_type=F32))
        cmax = None
        pvs = []
        for h in heads:
            cols = slice((h // 2) * LANES, (h // 2 + 1) * LANES)
            carry = carry_scr[h]
            w = jnp.where(mask, jnp.exp(lbs[h] + css[h][:, :tk] + carry), 0.0).astype(BF16)
            pvs.append(jnp.dot(w, v_ref[0, pl.ds(off, tk), cols], preferred_element_type=F32))
            carry = carry + css[h][:, tk:]
            carry_scr[h] = carry
            cmax = carry if cmax is None else jnp.maximum(cmax, carry)
        for p in range(npair):
            acc_scr[p] += jnp.where(lane < SB_DIM, pvs[2 * p], pvs[2 * p + 1])
        return j - 1, jnp.max(cmax)

    lax.while_loop(cond, body, (j_start, jnp.float32(0.0)))
    for p in range(npair):
        o_ref[0, :, p * LANES:(p + 1) * LANES] = acc_scr[p].astype(o_ref.dtype)


def _sb_attn_call(q, k, v, q_off):
    b, tq_all, _ = q.shape
    tk_all = k.shape[1]
    nq, nk = tq_all // ATTN_BLOCK, tk_all // ATTN_BLOCK
    qspec = pl.BlockSpec((1, ATTN_BLOCK, SB_W), lambda bi, i: (bi, i, 0))
    kspec = pl.BlockSpec((1, tk_all, SB_W), lambda bi, i: (bi, 0, 0))
    return pl.pallas_call(
        functools.partial(_sb_attn_kernel, q_off=q_off, nk=nk),
        grid=(b, nq), in_specs=[qspec, kspec, kspec], out_specs=qspec,
        out_shape=jax.ShapeDtypeStruct((b, tq_all, SB_W), BF16),
        scratch_shapes=[pltpu.VMEM((SB_HEADS, ATTN_BLOCK, ATTN_BLOCK), F32),
                        pltpu.VMEM((SB_W // LANES, ATTN_BLOCK, LANES), F32)],
        compiler_params=_cparams(("parallel", "arbitrary")), name="sb_attn")(q, k, v)


def _half_sizes(c):
    out, m = [], c // 2
    while m >= 1:
        out.append(m)
        m //= 2
    return out


def _block_ref_rows(b, m):
    c, dk = b.shape
    if 2 * m >= SUBLANES:
        pieces = [jnp.broadcast_to(b[a + m - 1:a + m, :], (2 * m, dk)) for a in range(0, c, 2 * m)]
        return pieces[0] if len(pieces) == 1 else jnp.concatenate(pieces, axis=0)
    row8 = lax.broadcasted_iota(jnp.int32, (SUBLANES, dk), 0)
    tiles = []
    for a in range(0, c, SUBLANES):
        tile = None
        for off in range(0, SUBLANES, 2 * m):
            r = jnp.broadcast_to(b[a + off + m - 1:a + off + m, :], (SUBLANES, dk))
            tile = r if tile is None else jnp.where(row8 >= off, r, tile)
        tiles.append(tile)
    return jnp.concatenate(tiles, axis=0)


def _gla_kernel(q_ref, k_ref, la_ref, v_ref, s0_ref, o_ref, st_ref, *, n_valid):
    c = GLA_CHUNK
    ci = pl.program_id(1)

    @pl.when(ci == 0)
    def _():
        st_ref[...] = s0_ref[...]

    r_iota = lax.broadcasted_iota(jnp.int32, (c, c), 0)
    c_iota = lax.broadcasted_iota(jnp.int32, (c, c), 1)
    tril = (r_iota >= c_iota).astype(BF16)
    level = jnp.where(c_iota < r_iota, 31 - lax.clz(r_iota ^ c_iota), jnp.where(c_iota == r_iota, -1, -2))
    valid = (ci * c + lax.broadcasted_iota(jnp.int32, (c, GLA_DK), 0)) < n_valid

    heads = range(GLA_HEADS)
    ks = [slice(h * GLA_DK, (h + 1) * GLA_DK) for h in heads]
    vs = [slice(h * GLA_DV, (h + 1) * GLA_DV) for h in heads]
    k = [jnp.where(valid, k_ref[0, :, ks[h]], 0.0) for h in heads]
    q = [q_ref[0, :, ks[h]] for h in heads]
    b = []
    for h in heads:
        la = jnp.where(valid, la_ref[0, :, ks[h]], 0.0)
        l1 = la.astype(BF16)
        r1 = la - l1.astype(F32)
        l2 = r1.astype(BF16)
        l3 = (r1 - l2.astype(F32)).astype(BF16)
        b.append(jnp.dot(tril, l1, preferred_element_type=F32) + jnp.dot(tril, l2, preferred_element_type=F32)
                 + jnp.dot(tril, l3, preferred_element_type=F32))
    inter = [lax.dot_general((q[h] * jnp.exp(b[h])).astype(BF16), st_ref[0, h].astype(BF16), _NT,
                             preferred_element_type=F32) for h in heads]
    scores = [jnp.where(level == -1, lax.dot_general(q[h].astype(BF16), k[h].astype(BF16), _NT,
                                                     preferred_element_type=F32), 0.0) for h in heads]
    for m in _half_sizes(c):
        for h in heads:
            ref_rows = _block_ref_rows(b[h], m)
            qm = (q[h] * jnp.exp(jnp.minimum(b[h] - ref_rows, 0.0))).astype(BF16)
            km = (k[h] * jnp.exp(jnp.minimum(ref_rows - b[h], 0.0))).astype(BF16)
            scores[h] = jnp.where(level == m.bit_length() - 1,
                                  lax.dot_general(qm, km, _NT, preferred_element_type=F32), scores[h])
    for h in heads:
        v = v_ref[0, :, vs[h]]
        o_ref[0, :, vs[h]] = inter[h] + jnp.dot(scores[h].astype(BF16), v, preferred_element_type=F32)
        b_last = b[h][c - 1:c, :]
        kd = (k[h] * jnp.exp(b_last - b[h])).astype(BF16)
        st_ref[0, h] = st_ref[0, h] * jnp.exp(b_last) + lax.dot_general(v, kd, _TN, preferred_element_type=F32)


def _gla_call(q, k, la, v, s0t, n_valid):
    b, t, _ = q.shape
    c = GLA_CHUNK
    kspec = pl.BlockSpec((1, c, GLA_KW), lambda bi, ci: (bi, ci, 0))
    vspec = pl.BlockSpec((1, c, GLA_VW), lambda bi, ci: (bi, ci, 0))
    sspec = pl.BlockSpec((1, GLA_HEADS, GLA_DV, GLA_DK), lambda bi, ci: (bi, 0, 0, 0))
    return pl.pallas_call(
        functools.partial(_gla_kernel, n_valid=n_valid),
        grid=(b, t // c), in_specs=[kspec, kspec, kspec, vspec, sspec],
        out_specs=[vspec, sspec],
        out_shape=[jax.ShapeDtypeStruct((b, t, GLA_VW), F32),
                   jax.ShapeDtypeStruct((b, GLA_HEADS, GLA_DV, GLA_DK), F32)],
        compiler_params=_cparams(("parallel", "arbitrary")), name="gla")(q, k, la, v, s0t)


def _merge_kernel(h_ref, sb_ref, go_ref, rb_ref, ga_ref, gb_ref, gn_ref, wpa_ref, wpb_ref, wo_ref,
                  g1_ref, b1_ref, out_ref):
    go = go_ref[...]
    parts = []
    for h in range(GLA_HEADS):
        x = go[:, h * GLA_DV:(h + 1) * GLA_DV]
        ms = jnp.mean(x * x, axis=-1, keepdims=True)
        parts.append(x * lax.rsqrt(ms + RMS_EPS) * gn_ref[...])
    o = jnp.concatenate(parts, axis=-1)
    gated = (o * rb_ref[...]).astype(BF16)
    gla_y = jnp.dot(gated, wpb_ref[...], preferred_element_type=F32)
    sb_y = jnp.dot(sb_ref[...], wpa_ref[...], preferred_element_type=F32)
    merged = ga_ref[...] * sb_y + gb_ref[...] * gla_y
    y = jnp.dot(merged.astype(BF16), wo_ref[...], preferred_element_type=F32)
    out_ref[...] = _layer_norm(DN_ALPHA * h_ref[...] + y, g1_ref[...], b1_ref[...])


def _merge_call(h, sb, go, rb, ga, gb, gn, wpa, wpb, wo, g1, b1):
    n, d = h.shape
    tm = min(_row_tile(n), 256)
    row = lambda w: pl.BlockSpec((tm, w), lambda i: (i, 0))
    full = lambda a: pl.BlockSpec(a.shape, lambda i: (0,) * a.ndim)
    args = (h, sb, go, rb, ga, gb, gn, wpa, wpb, wo, g1, b1)
    return pl.pallas_call(
        _merge_kernel, grid=(n // tm,),
        in_specs=[row(d), row(SB_W), row(GLA_VW), row(GLA_VW), row(d), row(d)] + [full(a) for a in args[6:]],
        out_specs=row(d), out_shape=jax.ShapeDtypeStruct((n, d), F32),
        compiler_params=_cparams(("parallel",)), name="merge")(*args)


def _sort_network(n):
    def merge(lo, hi, r):
        step = r * 2
        if step < hi - lo:
            yield from merge(lo, hi, step)
            yield from merge(lo + r, hi, step)
            yield from [(i, i + r) for i in range(lo + r, hi - r, step)]
        else:
            yield (lo, lo + r)

    def sort(lo, hi):
        if hi - lo >= 1:
            mid = lo + (hi - lo) // 2
            yield from sort(lo, mid)
            yield from sort(mid + 1, hi)
            yield from merge(lo, hi, 1)

    return tuple(sort(0, n - 1))


def _bitonic_merge_network(n):
    out, k = [], n // 2
    while k >= 1:
        out.extend((i, i + k) for i in range(n) if not i & k)
        k //= 2
    return tuple(out)


_SORT_TOPK = _sort_network(PEER_TOPK)
_MERGE_TOPK = _bitonic_merge_network(PEER_TOPK)


def _exchange(v, net):
    v = list(v)
    for i, j in net:
        hi, lo = jnp.maximum(v[i], v[j]), jnp.minimum(v[i], v[j])
        v[i], v[j] = hi, lo
    return v


def _merge_top(x, y):
    n = PEER_TOPK
    return _exchange([jnp.maximum(x[k], y[n - 1 - k]) for k in range(n)], _MERGE_TOPK)


def _exchange_lockstep(vs, net):
    vs = [list(v) for v in vs]
    for i, j in net:
        for v in vs:
            hi, lo = jnp.maximum(v[i], v[j]), jnp.minimum(v[i], v[j])
            v[i], v[j] = hi, lo
    return vs


def _column_top(ss):
    n = PEER_TOPK
    vs = _exchange_lockstep([[s[SUBLANES * k:SUBLANES * (k + 1), :] for k in range(n)] for s in ss], _SORT_TOPK)
    for shift in (4, 2, 1):
        vs = _exchange_lockstep([[jnp.maximum(v[k], pltpu.roll(v[n - 1 - k], shift, 0)) for k in range(n)]
                                 for v in vs], _MERGE_TOPK)
    return vs


_CAND_COLS = tuple(PEER_TOPK // (a + 1) for a in range(PEER_TOPK // 2))


def _ranked_top(s):
    rows = s.shape[0]
    iota = lax.broadcasted_iota(jnp.int32, s.shape, 0)
    rank = jnp.full(s.shape, float(PEER_TOPK), F32)
    vals = []
    for k in range(PEER_TOPK):
        m = jnp.max(s, axis=0, keepdims=True)
        first = jnp.min(jnp.where(s == m, iota, rows), axis=0, keepdims=True)
        hit = iota == first
        rank = jnp.where(hit, float(k), rank)
        s = jnp.where(hit, -jnp.inf, s)
        vals.append(m)
    return rank, vals


def _peer_kernel(h_ref, wqt_ref, keys_ref, u_ref, vt_ref, g2_ref, b2_ref, out_ref,
                 qt_scr, s_scr, e1_scr, e2_scr, top_scr, th_scr, stat_scr, flag_scr, acc_scr, hbt_scr, *, nc):
    ci = pl.program_id(1)
    tm = h_ref.shape[0]
    nkeys = keys_ref.shape[1]

    @pl.when(ci == 0)
    def _():
        hbt_scr[...] = h_ref[...].T.astype(BF16)
        qt_scr[...] = jnp.dot(wqt_ref[...], hbt_scr[...], preferred_element_type=F32)
        top_scr[...] = jnp.zeros_like(top_scr)
        flag_scr[...] = jnp.zeros_like(flag_scr)
        sub = lax.broadcasted_iota(jnp.int32, (SUBLANES, tm), 0)

        def per_head_scores(hd, carry):
            base = pl.multiple_of(hd * (2 * PEER_HALF), 2 * PEER_HALF)
            ss = [jnp.dot(keys_ref[side], qt_scr[pl.ds(base + side * PEER_HALF, PEER_HALF), :],
                          preferred_element_type=F32) for side in range(2)]
            tops = _column_top(ss)
            tie = flag_scr[0:SUBLANES, :]
            for side in range(2):
                s, top = ss[side], tops[side]
                s_scr[2 * hd + side] = s
                at_least = jnp.sum(jnp.where(s >= top[PEER_TOPK - 1][0:1, :], 1.0, 0.0), axis=0, keepdims=True)
                tie = jnp.where(at_least != float(PEER_TOPK), 1.0, tie)
                for k in range(PEER_TOPK):
                    top_scr[side, k] = jnp.where(sub == hd, top[k], top_scr[side, k])
                    if k:
                        tie = jnp.where(top[k - 1] == top[k], 1.0, tie)
            flag_scr[0:SUBLANES, :] = tie
            return carry

        lax.fori_loop(0, PEER_HEADS, per_head_scores, 0)

        t1 = [top_scr[0, k] for k in range(PEER_TOPK)]
        t2 = [top_scr[1, k] for k in range(PEER_TOPK)]
        neg = jnp.full((SUBLANES, tm), -jnp.inf, F32)
        lists = [[t1[a] + t2[b] for b in range(nb)] + [neg] * (PEER_TOPK - nb) for a, nb in enumerate(_CAND_COLS)]
        half = PEER_TOPK // 2
        lists.append([t1[a] + t2[0] for a in range(half, PEER_TOPK)] + [neg] * half)
        while len(lists) > 1:
            lists = [_merge_top(lists[i], lists[i + 1]) if i + 1 < len(lists) else lists[i]
                     for i in range(0, len(lists), 2)]
        cv = lists[0]
        tau = cv[PEER_TOPK - 1]
        zsum = jnp.exp(cv[0] - cv[0])
        for k in range(1, PEER_TOPK):
            zsum = zsum + jnp.exp(cv[k] - cv[0])
        stat_scr[1] = zsum
        stat_scr[2] = t1[0]
        stat_scr[3] = t2[0]
        n_reach = jnp.zeros((SUBLANES, tm), F32)
        for a in range(PEER_TOPK):
            th = jnp.full((SUBLANES, tm), jnp.inf, F32)
            for b in range(PEER_TOPK // (a + 1)):
                reach = t1[a] + t2[b] >= tau
                th = jnp.where(reach, t2[b], th)
                n_reach = n_reach + jnp.where(reach, 1.0, 0.0)
            th_scr[a] = th
        any_tie = jnp.max(jnp.where(n_reach != float(PEER_TOPK), 1.0, flag_scr[0:SUBLANES, :])) > 0.0

        @pl.when(jnp.logical_not(any_tie))
        def _():
            def per_head(hd, carry):
                zr = stat_scr[1, pl.ds(hd, 1), :]
                m1 = stat_scr[2, pl.ds(hd, 1), :]
                m2 = stat_scr[3, pl.ds(hd, 1), :]
                s1 = s_scr[2 * hd]
                e1_scr[hd] = jnp.exp(s1 - m1) * (0.5 / zr)
                e2_scr[hd] = jnp.exp(s_scr[2 * hd + 1] - m2)
                thr = jnp.full((nkeys, tm), jnp.inf, F32)
                for a in range(PEER_TOPK):
                    thr = jnp.where(s1 == top_scr[0, a, pl.ds(hd, 1), :], th_scr[a, pl.ds(hd, 1), :], thr)
                s_scr[2 * hd] = thr
                return carry

            lax.fori_loop(0, PEER_HEADS, per_head, 0)

        @pl.when(any_tie)
        def _():
            def per_head(hd, carry):
                s1 = s_scr[2 * hd]
                s2 = s_scr[2 * hd + 1]
                rank1, v1 = _ranked_top(s1)
                rank2, v2 = _ranked_top(s2)
                v2col = jnp.concatenate(v2, axis=0)
                cand = jnp.concatenate([v1[a] + v2col for a in range(PEER_TOPK)], axis=0)
                crank, cvals = _ranked_top(cand)
                taken = jnp.where(crank < float(PEER_TOPK), 1.0, 0.0)
                pairs = jnp.zeros((nkeys, tm), F32)
                for a in range(PEER_TOPK):
                    n_a = jnp.sum(taken[a * PEER_TOPK:(a + 1) * PEER_TOPK, :], axis=0, keepdims=True)
                    pairs = jnp.where(rank1 == float(a), n_a, pairs)
                zs = jnp.exp(cvals[0] - cvals[0])
                for k in range(1, PEER_TOPK):
                    zs = zs + jnp.exp(cvals[k] - cvals[0])
                e1_scr[hd] = jnp.exp(s1 - v1[0]) * (0.5 / zs)
                e2_scr[hd] = jnp.exp(s2 - v2[0])
                s_scr[2 * hd] = jnp.where(pairs > 0.0, 1.0 - pairs, jnp.inf)
                s_scr[2 * hd + 1] = jnp.where(rank2 < float(PEER_TOPK), -rank2, -jnp.inf)
                return carry

            lax.fori_loop(0, PEER_HEADS, per_head, 0)

        acc_scr[...] = jnp.zeros_like(acc_scr)

    piece = 2 * nkeys
    npiece = PEER_GROUP // 2

    def gate_rows(i1):
        w = None
        for hd in range(PEER_HEADS):
            thr = s_scr[2 * hd, pl.ds(i1, 1), :]
            e1r = e1_scr[hd, pl.ds(i1, 1), :]
            t = jnp.where(s_scr[2 * hd + 1] >= thr, e2_scr[hd] * e1r, 0.0)
            w = t if w is None else w + t
        return w

    for pi in range(npiece):
        rows = slice(pi * piece, (pi + 1) * piece)
        pre = jnp.dot(u_ref[rows, :], hbt_scr[...], preferred_element_type=F32)
        i1 = ci * PEER_GROUP + 2 * pi
        w = jnp.concatenate([gate_rows(i1), gate_rows(i1 + 1)], axis=0)
        act = pre * (1.0 + lax.erf(pre * (2.0 ** -0.5)))
        a = (act * w).astype(BF16)
        acc_scr[...] += jnp.dot(vt_ref[:, rows], a, preferred_element_type=F32)

    @pl.when(ci == nc - 1)
    def _():
        y = acc_scr[...].T
        out_ref[...] = _layer_norm(DN_ALPHA * h_ref[...] + y, g2_ref[...], b2_ref[...])


def _peer_call(h, wqt, keys, u, vt, g2, b2):
    n, d = h.shape
    nkeys = keys.shape[1]
    tm = min(_row_tile(n), 256)
    g = PEER_GROUP
    nc = nkeys // g
    hspec = pl.BlockSpec((tm, d), lambda i, c: (i, 0))
    full = lambda a: pl.BlockSpec(a.shape, lambda i, c: (0,) * a.ndim)
    assert nkeys == PEER_TOPK * SUBLANES and PEER_HEADS == SUBLANES
    dense = pltpu.VMEM((PEER_HEADS, nkeys, tm), F32)
    return pl.pallas_call(
        functools.partial(_peer_kernel, nc=nc), grid=(n // tm, nc),
        in_specs=[hspec, full(wqt), full(keys),
                  pl.BlockSpec((g * nkeys, d), lambda i, c: (c, 0)),
                  pl.BlockSpec((d, g * nkeys), lambda i, c: (0, c)), full(g2), full(b2)],
        out_specs=hspec, out_shape=jax.ShapeDtypeStruct((n, d), F32),
        scratch_shapes=[pltpu.VMEM((PEER_HEADS * 2 * PEER_HALF, tm), F32),
                        pltpu.VMEM((2 * PEER_HEADS, nkeys, tm), F32), dense, dense,
                        pltpu.VMEM((2, PEER_TOPK, SUBLANES, tm), F32), pltpu.VMEM((PEER_TOPK, SUBLANES, tm), F32),
                        pltpu.VMEM((4, SUBLANES, tm), F32), pltpu.VMEM((2 * SUBLANES, tm), F32),
                        pltpu.VMEM((d, tm), F32), pltpu.VMEM((d, tm), BF16)],
        compiler_params=_cparams(("parallel", "arbitrary")), name="peer")(h, wqt, keys, u, vt, g2, b2)


def _round_up(x, m):
    return -(-x // m) * m


def _pad_rows(x, rows):
    return jnp.pad(x, ((0, 0), (0, rows - x.shape[1]), (0, 0)))


def _pad_flat(x, rows):
    return jnp.pad(x, ((0, rows - x.shape[0]), (0, 0)))


def kernel(x_prompt, x_sample, cache_k, cache_v, state_gla, meta, ln_in_g, ln_in_b, w_in, w_gk2, b_gk,
           gla_norm_g, w_pa, w_pb, w_o, ln1_g, ln1_b, peer_wq, peer_subkeys, peer_u, peer_v, ln2_g, ln2_b):
    bp, s_len, d = x_prompt.shape
    bd, t_new, _ = x_sample.shape
    past = cache_k.shape[2]
    n_len = N_META + s_len
    keep = min(s_len, past)
    lp = _round_up(n_len, ATTN_BLOCK)
    ts = _round_up(t_new, ATTN_BLOCK)
    tg = _round_up(t_new, GLA_CHUNK)
    tk_s = _round_up(N_META + past + t_new, ATTN_BLOCK)
    row = lambda a: a.reshape(1, -1)

    meta_rows = jnp.broadcast_to(meta[None].astype(x_prompt.dtype), (bp, N_META, d))
    xp = _pad_rows(jnp.concatenate([meta_rows, x_prompt], axis=1), lp).reshape(bp * lp, d)
    ns = bd * t_new
    ns_p = _round_up(ns, LANES)
    xs = _pad_flat(x_sample.reshape(ns, d), ns_p)
    hp = _ln_call(xp, ln_in_g, ln_in_b)
    hs = _ln_call(xs, ln_in_g, ln_in_b)

    outs = {k: [] for k in ("nk_p", "nv_p", "st_p", "nk_s", "nv_s", "st_s")}
    for l in range(DEPTH):
        glr0 = _SEC[7]
        w_main = jnp.concatenate([w_in[l][:, :glr0], w_in[l][:, glr0 + GLA_RANK:]], axis=1).astype(BF16)
        w_glr = jnp.pad(w_in[l][:, glr0:glr0 + GLA_RANK], ((0, 0), (0, LANES - GLA_RANK))).astype(BF16)
        w_gk2p = jnp.pad(w_gk2[l], ((0, LANES - GLA_RANK), (0, 0)))
        merge_w = (row(gla_norm_g[l]), w_pa[l].astype(BF16), w_pb[l].astype(BF16), w_o[l].astype(BF16),
                   row(ln1_g[l]), row(ln1_b[l]))
        peer_w = (peer_wq[l].T.astype(BF16), peer_subkeys[l], peer_u[l].astype(BF16),
                  peer_v[l].T.astype(BF16), row(ln2_g[l]), row(ln2_b[l]))

        qa, kab, vab, ka, va, qb, kb, vb, la, rb, ga, gb = _inproj_call(hp, w_main, w_glr, w_gk2p, row(b_gk[l]))
        b3 = lambda a: a.reshape(bp, lp, a.shape[-1])
        sb = _sb_attn_call(b3(qa), b3(kab), b3(vab), 0)
        s0t = jnp.zeros((bp, GLA_HEADS, GLA_DV, GLA_DK), F32)
        go, stp = _gla_call(b3(qb), b3(kb), b3(la), b3(vb), s0t, n_len)
        hp = _merge_call(hp, sb.reshape(bp * lp, SB_W), go.reshape(bp * lp, GLA_VW), rb, ga, gb, *merge_w)
        hp = _peer_call(hp, *peer_w)
        ka3 = ka.reshape(bp, lp, SB_HEADS, SB_DIM)
        va3 = va.reshape(bp, lp, SB_HEADS, SB_DIM)
        outs["nk_p"].append(ka3[:, n_len - keep:n_len])
        outs["nv_p"].append(va3[:, n_len - keep:n_len])
        outs["st_p"].append(jnp.swapaxes(stp, 2, 3).astype(state_gla.dtype))

        qa2, _, _, ka2, va2, qb2, kb2, vb2, la2, rb2, ga2, gb2 = _inproj_call(
            hs, w_main, w_glr, w_gk2p, row(b_gk[l]))
        s3 = lambda a: a[:ns].reshape(bd, t_new, a.shape[-1])
        flat = lambda a: _pad_flat(a[:, :t_new].reshape(ns, a.shape[-1]), ns_p)
        ka2n = s3(ka2)
        va2n = s3(va2)
        meta_k = jnp.broadcast_to(ka3[:1, :N_META].reshape(1, N_META, SB_W), (bd, N_META, SB_W))
        meta_v = jnp.broadcast_to(va3[:1, :N_META].reshape(1, N_META, SB_W), (bd, N_META, SB_W))
        k_all = jnp.concatenate([meta_k, cache_k[l].reshape(bd, past, SB_W), ka2n], axis=1)
        v_all = jnp.concatenate([meta_v, cache_v[l].reshape(bd, past, SB_W), va2n], axis=1)
        sb2 = _sb_attn_call(_pad_rows(s3(qa2), ts), _pad_rows(k_all, tk_s).astype(BF16),
                            _pad_rows(v_all, tk_s).astype(BF16), N_META + past)
        g3 = lambda a: _pad_rows(s3(a), tg)
        go2, sts = _gla_call(g3(qb2), g3(kb2), g3(la2), g3(vb2), jnp.swapaxes(state_gla[l], 2, 3), t_new)
        hs = _merge_call(hs, flat(sb2), flat(go2), rb2, ga2, gb2, *merge_w)
        hs = _peer_call(hs, *peer_w)
        outs["nk_s"].append(ka2n.reshape(bd, t_new, SB_HEADS, SB_DIM))
        outs["nv_s"].append(va2n.reshape(bd, t_new, SB_HEADS, SB_DIM))
        outs["st_s"].append(jnp.swapaxes(sts, 2, 3).astype(state_gla.dtype))

    y_prompt = hp.reshape(bp, lp, d)[:, N_META:n_len]
    y_sample = hs[:ns].reshape(bd, t_new, d)
    return (y_prompt, y_sample, jnp.stack(outs["nk_p"]), jnp.stack(outs["nv_p"]), jnp.stack(outs["st_p"]),
            jnp.stack(outs["nk_s"]), jnp.stack(outs["nv_s"]), jnp.stack(outs["st_s"]))
```
